```python
import jax, jax.numpy as jnp
from jax import lax
import numpy as np

D_MODEL = 1024
BATCH = 16
SEQ = 2048
DEPTH = 1
DEC_BATCH = 32
DEC_SEQ = 1
PAST_LEN = 16384
PAGE_SIZE = 128

HEAD_DIM = 64
NSA_HEADS = 8
NSA_GROUPS = 2
NSA_REP = NSA_HEADS // NSA_GROUPS
NSA_CMP_LEN = 32
NSA_CMP_STRIDE = 16
NSA_CMP_HIDDEN = 256
NSA_SEL_BLOCK = 64
NSA_TOPN = 16
NSA_WINDOW = 512
MOBA_HEADS = 8
MOBA_BLOCK = 256
MOBA_TOPK = 3
MEM_LEN = 256
MEM_HEADS = 4
MEM_HEAD_DIM = 128
N_BRANCHES = 3
N_GROUPS = 4
EXPERTS_PER_GROUP = 8
N_EXPERTS = N_GROUPS * EXPERTS_PER_GROUP
TOP_K_IN_GROUP = 2
EXPERT_FF = 512
MOE_BLOCK = 128
ROPE_THETA = 10000.0
NORM_EPS = 1e-6
Q_BLOCK = 128

NSA_Q_W = NSA_HEADS * HEAD_DIM
NSA_KV_W = NSA_GROUPS * HEAD_DIM
NSA_GATE_W = NSA_HEADS * 3
MOBA_W = MOBA_HEADS * HEAD_DIM
MEM_W = MEM_HEADS * MEM_HEAD_DIM
PROJ_WIDTH = NSA_Q_W + NSA_GATE_W + 6 * NSA_KV_W + 3 * MOBA_W + MEM_W

kernel_name = "nsa_moba_memory_hmoe_decoder_step"


def rms_norm(x, gain):
    xf = x.astype(jnp.float32)
    y = xf * lax.rsqrt(jnp.mean(xf * xf, axis=-1, keepdims=True) + NORM_EPS)
    return (y * gain.astype(jnp.float32)).astype(x.dtype)


def rotary(x, pos):
    half = x.shape[-1] // 2
    inv_freq = ROPE_THETA ** (-jnp.arange(half, dtype=jnp.float32) / half)
    ang = pos.astype(jnp.float32)[:, None] * inv_freq[None, :]
    cos = jnp.cos(ang)[None, :, None, :]
    sin = jnp.sin(ang)[None, :, None, :]
    xf = x.astype(jnp.float32)
    x1, x2 = xf[..., :half], xf[..., half:]
    return jnp.concatenate([x1 * cos - x2 * sin, x2 * cos + x1 * sin], axis=-1).astype(x.dtype)


def masked_softmax(scores, mask):
    s = jnp.where(mask, scores, -jnp.inf)
    m = jnp.max(s, axis=-1, keepdims=True)
    e = jnp.exp(s - jnp.where(jnp.isfinite(m), m, 0.0))
    d = jnp.sum(e, axis=-1, keepdims=True)
    return e / jnp.where(d > 0, d, 1.0)


def contiguous_fetch(rows):
    total = rows.shape[1]
    def fetch(pos, hidx):
        return jax.vmap(lambda r, p, g: r[p, g])(rows, jnp.clip(pos, 0, total - 1), hidx)
    return fetch


def paged_fetch(pool, page_table, new_rows):
    n_seq, n_pages = page_table.shape
    past_len = n_pages * PAGE_SIZE
    n_new = new_rows.shape[1]
    total = past_len + n_new
    def fetch(pos, hidx):
        pos = jnp.clip(pos, 0, total - 1)
        in_past = pos < past_len
        pp = jnp.minimum(pos, past_len - 1)
        logical = (pp // PAGE_SIZE).reshape(n_seq, -1)
        phys = jnp.take_along_axis(page_table, logical, axis=1).reshape(pos.shape)
        from_pool = pool[phys, pp % PAGE_SIZE, hidx]
        npos = jnp.clip(pos - past_len, 0, n_new - 1)
        from_new = jax.vmap(lambda r, p, g: r[p, g])(new_rows, npos, hidx)
        return jnp.where(in_past[..., None], from_pool, from_new)
    return fetch


def nsa_compress(rows, w1, b1, w2):
    B, L, G, dh = rows.shape
    ratio = NSA_CMP_LEN // NSA_CMP_STRIDE
    n_chunks = L // NSA_CMP_STRIDE
    nc = n_chunks - ratio + 1
    chunks = rows[:, : n_chunks * NSA_CMP_STRIDE].reshape(B, n_chunks, NSA_CMP_STRIDE, G, dh)
    chunks = chunks.transpose(0, 1, 3, 2, 4).reshape(B, n_chunks, G, NSA_CMP_STRIDE * dh)
    w1r = w1.reshape(ratio, NSA_CMP_STRIDE * dh, NSA_CMP_HIDDEN)
    hid = b1 + sum(jnp.einsum("bcgx,xh->bcgh", chunks[:, m:m + nc], w1r[m]) for m in range(ratio))
    return jnp.einsum("bcgh,hd->bcgd", jax.nn.gelu(hid), w2)


def nsa_static_maps(total_len):
    ratio_c = NSA_CMP_LEN // NSA_CMP_STRIDE
    ratio_s = NSA_SEL_BLOCK // NSA_CMP_STRIDE
    nc = total_len // NSA_CMP_STRIDE - ratio_c + 1
    nsb = max(-(-total_len // NSA_SEL_BLOCK), NSA_TOPN)
    mat = np.zeros((nc, nsb), np.float32)
    j = np.arange(nsb)
    for m in range(ratio_s):
        for n in range(ratio_c):
            i = ratio_s * j + m - n
            ok = (i >= 0) & (i < nc)
            np.add.at(mat, (i[ok], j[ok]), 1.0)
    cmp_end = np.arange(nc) * NSA_CMP_STRIDE + NSA_CMP_LEN - 1
    return jnp.asarray(cmp_end, jnp.int32), jnp.asarray(mat)


def nsa_attend(q, gates, q_pos, cmp_k, cmp_v, cmp_end, imp_map, fetch_k, fetch_v, win_k, win_v, win_pos):
    B, T = q.shape[:2]
    f32 = jnp.float32
    scale = HEAD_DIM ** -0.5
    qg = q.reshape(B, T, NSA_GROUPS, NSA_REP, HEAD_DIM)
    qp = q_pos[:, None]
    s = jnp.einsum("btgrd,bcgd->btgrc", qg, cmp_k, preferred_element_type=f32) * scale
    p_c = masked_softmax(s, (cmp_end[None, :] <= qp)[None, :, None, None, :])
    o_c = jnp.einsum("btgrc,bcgd->btgrd", p_c.astype(cmp_v.dtype), cmp_v)
    imp = jnp.einsum("btgc,cj->btgj", p_c.sum(axis=3), imp_map)
    j = jnp.arange(imp_map.shape[1])[None, :]
    cur = (q_pos // NSA_SEL_BLOCK)[:, None]
    visible = (j <= cur)[None, :, None, :]
    forced = ((j == 0) | (j == cur) | (j == cur - 1))[None, :, None, :]
    score = jnp.where(visible, jnp.where(forced, jnp.inf, imp), -jnp.inf)
    top_s, top_j = lax.top_k(score, NSA_TOPN)
    kpos = (top_j[..., None] * NSA_SEL_BLOCK + jnp.arange(NSA_SEL_BLOCK)).reshape(B, T, NSA_GROUPS, -1)
    kmask = jnp.repeat(top_s > -jnp.inf, NSA_SEL_BLOCK, axis=-1) & (kpos <= q_pos[None, :, None, None])
    gidx = jnp.broadcast_to(jnp.arange(NSA_GROUPS)[None, None, :, None], kpos.shape)
    ks = fetch_k(kpos, gidx)
    vs = fetch_v(kpos, gidx)
    s = jnp.einsum("btgrd,btgkd->btgrk", qg, ks, preferred_element_type=f32) * scale
    p_s = masked_softmax(s, kmask[:, :, :, None, :])
    o_s = jnp.einsum("btgrk,btgkd->btgrd", p_s.astype(vs.dtype), vs)
    kp = win_pos[None, :]
    wmask = ((kp <= qp) & (kp > qp - NSA_WINDOW) & (kp >= 0))[None, :, None, None, :]
    s = jnp.einsum("btgrd,bkgd->btgrk", qg, win_k, preferred_element_type=f32) * scale
    p_w = masked_softmax(s, wmask)
    o_w = jnp.einsum("btgrk,bkgd->btgrd", p_w.astype(win_v.dtype), win_v)
    g = gates.reshape(B, T, NSA_GROUPS, NSA_REP, 3)
    o = g[..., 0:1] * o_c + g[..., 1:2] * o_s + g[..., 2:3] * o_w
    return o.reshape(B, T, NSA_Q_W)


def moba_block_means(k):
    B, L, H, dh = k.shape
    nb = L // MOBA_BLOCK
    m = k[:, : nb * MOBA_BLOCK].reshape(B, nb, MOBA_BLOCK, H, dh).astype(jnp.float32).mean(axis=2)
    m = jnp.pad(m, ((0, 0), (0, max(MOBA_TOPK - nb, 0)), (0, 0), (0, 0)))
    return m.astype(k.dtype)


def moba_attend(q, q_pos, means, fetch_k, fetch_v):
    B, T, H, dh = q.shape
    f32 = jnp.float32
    gate = jnp.einsum("bthd,bjhd->bthj", q, means, preferred_element_type=f32)
    cur = q_pos // MOBA_BLOCK
    past = (jnp.arange(means.shape[1])[None, :] < cur[:, None])[None, :, None, :]
    top_s, top_j = lax.top_k(jnp.where(past, gate, -jnp.inf), MOBA_TOPK)
    blocks = jnp.concatenate([top_j, jnp.broadcast_to(cur[None, :, None, None], (B, T, H, 1))], axis=-1)
    bvalid = jnp.concatenate([top_s > -jnp.inf, jnp.ones((B, T, H, 1), bool)], axis=-1)
    kpos = (blocks[..., None] * MOBA_BLOCK + jnp.arange(MOBA_BLOCK)).reshape(B, T, H, -1)
    kmask = jnp.repeat(bvalid, MOBA_BLOCK, axis=-1) & (kpos <= q_pos[None, :, None, None])
    hidx = jnp.broadcast_to(jnp.arange(H)[None, None, :, None], kpos.shape)
    ks = fetch_k(kpos, hidx)
    vs = fetch_v(kpos, hidx)
    s = jnp.einsum("bthd,bthkd->bthk", q, ks, preferred_element_type=f32) * (dh ** -0.5)
    p = masked_softmax(s, kmask)
    o = jnp.einsum("bthk,bthkd->bthd", p.astype(vs.dtype), vs)
    return o.reshape(B, T, H * dh)


def memory_kv(mem, lw):
    B, M, _ = mem.shape
    hm = rms_norm(mem, lw["norm_mem"])
    kv = jnp.einsum("bmd,dc->bmc", hm, lw["w_mem_kv"]).reshape(B, M, 2, MEM_HEADS, MEM_HEAD_DIM)
    return kv[:, :, 0], kv[:, :, 1]


def memory_attend(q, mk, mv):
    B, T = q.shape[:2]
    s = jnp.einsum("bthd,bmhd->bthm", q, mk, preferred_element_type=jnp.float32) * (MEM_HEAD_DIM ** -0.5)
    p = jax.nn.softmax(s, axis=-1)
    return jnp.einsum("bthm,bmhd->bthd", p.astype(mv.dtype), mv).reshape(B, T, MEM_W)


def mixer_inputs(h, pos, w_in):
    B, T, _ = h.shape
    widths = (NSA_Q_W, NSA_GATE_W, NSA_KV_W, NSA_KV_W, NSA_KV_W, NSA_KV_W, NSA_KV_W, NSA_KV_W,
              MOBA_W, MOBA_W, MOBA_W, MEM_W)
    cuts = np.cumsum(widths)[:-1].tolist()
    parts = jnp.split(jnp.einsum("btd,dc->btc", h, w_in), cuts, axis=-1)
    nsa_q, nsa_g, k_cmp, v_cmp, k_slc, v_slc, k_win, v_win, moba_q, moba_k, moba_v, mem_q = parts
    heads = lambda a, n, d: a.reshape(B, T, n, d)
    return (rotary(heads(nsa_q, NSA_HEADS, HEAD_DIM), pos),
            jax.nn.sigmoid(heads(nsa_g, NSA_HEADS, 3)),
            rotary(heads(k_cmp, NSA_GROUPS, HEAD_DIM), pos), heads(v_cmp, NSA_GROUPS, HEAD_DIM),
            rotary(heads(k_slc, NSA_GROUPS, HEAD_DIM), pos), heads(v_slc, NSA_GROUPS, HEAD_DIM),
            rotary(heads(k_win, NSA_GROUPS, HEAD_DIM), pos), heads(v_win, NSA_GROUPS, HEAD_DIM),
            rotary(heads(moba_q, MOBA_HEADS, HEAD_DIM), pos),
            rotary(heads(moba_k, MOBA_HEADS, HEAD_DIM), pos),
            heads(moba_v, MOBA_HEADS, HEAD_DIM),
            heads(mem_q, MEM_HEADS, MEM_HEAD_DIM))


def grouped_expert_ffn(h, expert, weight, w_gate, w_up, w_down):
    n_tok, d = h.shape
    n_assign = expert.size
    blk = min(MOE_BLOCK, n_assign)
    n_blocks = -(-n_assign // blk) + N_EXPERTS
    cap = n_blocks * blk
    flat_e = expert.reshape(-1)
    flat_t = jnp.repeat(jnp.arange(n_tok, dtype=jnp.int32), expert.shape[1])
    flat_w = weight.reshape(-1)
    order = jnp.argsort(flat_e)
    e_sorted = flat_e[order]
    counts = jnp.bincount(flat_e, length=N_EXPERTS)
    padded = (counts + blk - 1) // blk * blk
    pad_end = jnp.cumsum(padded)
    pad_start = pad_end - padded
    sort_start = jnp.cumsum(counts) - counts
    dest = pad_start[e_sorted] + jnp.arange(n_assign) - sort_start[e_sorted]
    slot_tok = jnp.full((cap,), n_tok, jnp.int32).at[dest].set(flat_t[order])
    slot_w = jnp.zeros((cap,), h.dtype).at[dest].set(flat_w[order].astype(h.dtype))
    blk_expert = jnp.minimum(jnp.searchsorted(pad_end, jnp.arange(n_blocks) * blk, side="right"), N_EXPERTS - 1)
    h_pad = jnp.concatenate([h, jnp.zeros((1, d), h.dtype)], axis=0)
    xs = h_pad[slot_tok].reshape(n_blocks, blk, d)
    def expert_block(args):
        xb, e = args
        return (jax.nn.silu(xb @ w_gate[e]) * (xb @ w_up[e])) @ w_down[e]
    ys = lax.map(expert_block, (xs, blk_expert)).reshape(cap, d)
    out = jnp.zeros((n_tok + 1, d), ys.dtype).at[slot_tok].add(ys * slot_w[:, None])
    return out[:n_tok]


def moe_block(x, lw):
    B, T, D = x.shape
    h = rms_norm(x, lw["norm_ffn"]).reshape(B * T, D)
    hf = h.astype(jnp.float32)
    g_logits = hf @ lw["w_group"].astype(jnp.float32) + lw["b_group"].astype(jnp.float32)
    g_prob = jax.nn.softmax(g_logits, axis=-1)
    grp = jnp.argmax(g_logits, axis=-1).astype(jnp.int32)
    e_logits = (hf @ lw["w_router"].astype(jnp.float32) + lw["b_router"].astype(jnp.float32)).reshape(-1, N_GROUPS, EXPERTS_PER_GROUP)
    e_in = jnp.take_along_axis(e_logits, grp[:, None, None], axis=1)[:, 0]
    top_p, top_i = lax.top_k(jax.nn.softmax(e_in, axis=-1), TOP_K_IN_GROUP)
    weight = top_p / jnp.sum(top_p, axis=-1, keepdims=True) * jnp.take_along_axis(g_prob, grp[:, None], axis=1)
    expert = grp[:, None] * EXPERTS_PER_GROUP + top_i.astype(jnp.int32)
    out = grouped_expert_ffn(h, expert, weight, lw["w_expert_gate"], lw["w_expert_up"], lw["w_expert_down"])
    return out.reshape(B, T, D)


def merge_and_ffn(x, h, o_nsa, o_moba, o_mem, lw):
    B, T, D = x.shape
    gates = jax.nn.sigmoid(jnp.einsum("btd,dc->btc", h, lw["w_merge_gate"]) + lw["b_merge_gate"]).reshape(B, T, N_BRANCHES, D)
    mixed = (gates[:, :, 0] * jnp.einsum("btc,cd->btd", o_nsa, lw["w_nsa_out"])
             + gates[:, :, 1] * jnp.einsum("btc,cd->btd", o_moba, lw["w_moba_out"])
             + gates[:, :, 2] * jnp.einsum("btc,cd->btd", o_mem, lw["w_mem_out"]))
    x = x + jnp.einsum("btd,de->bte", mixed, lw["w_out"])
    return x + moe_block(x, lw)


def prompt_layer(x, mem, lw):
    B, S, _ = x.shape
    pos = jnp.arange(S, dtype=jnp.int32)
    h = rms_norm(x, lw["norm_mix"])
    (nsa_q, nsa_g, k_cmp, v_cmp, k_slc, v_slc, k_win, v_win,
     moba_q, moba_k, moba_v, mem_q) = mixer_inputs(h, pos, lw["w_in"])
    mem_k, mem_v = memory_kv(mem, lw)
    c_k = nsa_compress(k_cmp, lw["cmp_w1_k"], lw["cmp_b1_k"], lw["cmp_w2_k"])
    c_v = nsa_compress(v_cmp, lw["cmp_w1_v"], lw["cmp_b1_v"], lw["cmp_w2_v"])
    cmp_end, imp_map = nsa_static_maps(S)
    means = moba_block_means(moba_k)
    band_pad = ((0, 0), (NSA_WINDOW, 0), (0, 0), (0, 0))
    kw_pad = jnp.pad(k_win, band_pad)
    vw_pad = jnp.pad(v_win, band_pad)
    n_qb = S // Q_BLOCK

    def query_block(i):
        b = i // n_qb
        s0 = (i % n_qb) * Q_BLOCK
        q_pos = s0 + jnp.arange(Q_BLOCK, dtype=jnp.int32)
        rows = lambda a: lax.dynamic_slice_in_dim(a[b], s0, Q_BLOCK, axis=0)[None]
        band = lambda a: lax.dynamic_slice_in_dim(a[b], s0, NSA_WINDOW + Q_BLOCK, axis=0)[None]
        seq = lambda a: a[b][None]
        win_pos = s0 - NSA_WINDOW + jnp.arange(NSA_WINDOW + Q_BLOCK, dtype=jnp.int32)
        o_n = nsa_attend(rows(nsa_q), rows(nsa_g), q_pos, seq(c_k), seq(c_v), cmp_end, imp_map,
                         contiguous_fetch(seq(k_slc)), contiguous_fetch(seq(v_slc)),
                         band(kw_pad), band(vw_pad), win_pos)
        o_m = moba_attend(rows(moba_q), q_pos, seq(means),
                          contiguous_fetch(seq(moba_k)), contiguous_fetch(seq(moba_v)))
        return o_n[0], o_m[0]

    o_nsa, o_moba = lax.map(query_block, jnp.arange(B * n_qb, dtype=jnp.int32))
    o_nsa = o_nsa.reshape(B, S, NSA_Q_W)
    o_moba = o_moba.reshape(B, S, MOBA_W)
    o_mem = memory_attend(mem_q, mem_k, mem_v)
    x = merge_and_ffn(x, h, o_nsa, o_moba, o_mem, lw)
    wb = min(NSA_WINDOW, S)
    new_state = (k_cmp, v_cmp, k_slc, v_slc, k_win[:, S - wb:], v_win[:, S - wb:],
                 moba_k, moba_v, mem_k, mem_v)
    return x, new_state


def sample_layer(x, c_cmp_k, c_cmp_v, c_slc_k, c_slc_v, c_win_k, c_win_v,
                 c_moba_k, c_moba_v, c_mem_k, c_mem_v, page_table, lw):
    B, T, _ = x.shape
    past_len = page_table.shape[1] * PAGE_SIZE
    total = past_len + T
    pos = past_len + jnp.arange(T, dtype=jnp.int32)
    h = rms_norm(x, lw["norm_mix"])
    (nsa_q, nsa_g, k_cmp, v_cmp, k_slc, v_slc, k_win, v_win,
     moba_q, moba_k, moba_v, mem_q) = mixer_inputs(h, pos, lw["w_in"])
    past = lambda pool: pool[page_table].reshape(B, past_len, *pool.shape[2:])
    c_k = nsa_compress(jnp.concatenate([past(c_cmp_k), k_cmp], axis=1),
                       lw["cmp_w1_k"], lw["cmp_b1_k"], lw["cmp_w2_k"])
    c_v = nsa_compress(jnp.concatenate([past(c_cmp_v), v_cmp], axis=1),
                       lw["cmp_w1_v"], lw["cmp_b1_v"], lw["cmp_w2_v"])
    cmp_end, imp_map = nsa_static_maps(total)
    moba_k_all = jnp.concatenate([past(c_moba_k), moba_k], axis=1)
    means = moba_block_means(moba_k_all)
    wb = c_win_k.shape[1]
    wk_all = jnp.concatenate([c_win_k, k_win], axis=1)
    wv_all = jnp.concatenate([c_win_v, v_win], axis=1)
    win_pos = past_len - wb + jnp.arange(wb + T, dtype=jnp.int32)
    o_nsa = nsa_attend(nsa_q, nsa_g, pos, c_k, c_v, cmp_end, imp_map,
                       paged_fetch(c_slc_k, page_table, k_slc), paged_fetch(c_slc_v, page_table, v_slc),
                       wk_all, wv_all, win_pos)
    o_moba = moba_attend(moba_q, pos, means, contiguous_fetch(moba_k_all),
                         paged_fetch(c_moba_v, page_table, moba_v))
    o_mem = memory_attend(mem_q, c_mem_k, c_mem_v)
    x = merge_and_ffn(x, h, o_nsa, o_moba, o_mem, lw)
    new_state = (k_cmp, v_cmp, k_slc, v_slc, wk_all[:, -wb:], wv_all[:, -wb:], moba_k, moba_v)
    return x, new_state


def setup_inputs(seed: int = 0) -> dict:
    key = jax.random.key(seed)
    ks = iter(jax.random.split(key, 64))
    f32 = jnp.float32
    D = D_MODEL

    def nrm(shape, scale=1.0):
        return jax.random.normal(next(ks), shape, f32) * scale

    n_pages = PAST_LEN // PAGE_SIZE
    n_pool = (DEC_BATCH * n_pages * 5) // 4
    win_buf = min(NSA_WINDOW, PAST_LEN)
    page_table = jax.random.permutation(next(ks), n_pool)[: DEC_BATCH * n_pages].reshape(DEC_BATCH, n_pages).astype(jnp.int32)
    cmp_in = NSA_CMP_LEN * HEAD_DIM
    return {
        "x_prompt": nrm((BATCH, SEQ, D)),
        "x_sample": nrm((DEC_BATCH, DEC_SEQ, D)),
        "mem_prompt": nrm((BATCH, MEM_LEN, D)),
        "cache_nsa_cmp_k": nrm((DEPTH, n_pool, PAGE_SIZE, NSA_GROUPS, HEAD_DIM)),
        "cache_nsa_cmp_v": nrm((DEPTH, n_pool, PAGE_SIZE, NSA_GROUPS, HEAD_DIM)),
        "cache_nsa_slc_k": nrm((DEPTH, n_pool, PAGE_SIZE, NSA_GROUPS, HEAD_DIM)),
        "cache_nsa_slc_v": nrm((DEPTH, n_pool, PAGE_SIZE, NSA_GROUPS, HEAD_DIM)),
        "cache_nsa_win_k": nrm((DEPTH, DEC_BATCH, win_buf, NSA_GROUPS, HEAD_DIM)),
        "cache_nsa_win_v": nrm((DEPTH, DEC_BATCH, win_buf, NSA_GROUPS, HEAD_DIM)),
        "cache_moba_k": nrm((DEPTH, n_pool, PAGE_SIZE, MOBA_HEADS, HEAD_DIM)),
        "cache_moba_v": nrm((DEPTH, n_pool, PAGE_SIZE, MOBA_HEADS, HEAD_DIM)),
        "cache_mem_k": nrm((DEPTH, DEC_BATCH, MEM_LEN, MEM_HEADS, MEM_HEAD_DIM)),
        "cache_mem_v": nrm((DEPTH, DEC_BATCH, MEM_LEN, MEM_HEADS, MEM_HEAD_DIM)),
        "page_table": page_table,
        "norm_mix": 1.0 + nrm((DEPTH, D), 0.01),
        "norm_mem": 1.0 + nrm((DEPTH, D), 0.01),
        "w_in": nrm((DEPTH, D, PROJ_WIDTH), D ** -0.5),
        "w_mem_kv": nrm((DEPTH, D, 2 * MEM_W), D ** -0.5),
        "cmp_w1_k": nrm((DEPTH, cmp_in, NSA_CMP_HIDDEN), cmp_in ** -0.5),
        "cmp_b1_k": nrm((DEPTH, NSA_CMP_HIDDEN), 0.01),
        "cmp_w2_k": nrm((DEPTH, NSA_CMP_HIDDEN, HEAD_DIM), NSA_CMP_HIDDEN ** -0.5),
        "cmp_w1_v": nrm((DEPTH, cmp_in, NSA_CMP_HIDDEN), cmp_in ** -0.5),
        "cmp_b1_v": nrm((DEPTH, NSA_CMP_HIDDEN), 0.01),
        "cmp_w2_v": nrm((DEPTH, NSA_CMP_HIDDEN, HEAD_DIM), NSA_CMP_HIDDEN ** -0.5),
        "w_nsa_out": nrm((DEPTH, NSA_Q_W, D), NSA_Q_W ** -0.5),
        "w_moba_out": nrm((DEPTH, MOBA_W, D), MOBA_W ** -0.5),
        "w_mem_out": nrm((DEPTH, MEM_W, D), MEM_W ** -0.5),
        "w_merge_gate": nrm((DEPTH, D, N_BRANCHES * D), D ** -0.5),
        "b_merge_gate": nrm((DEPTH, N_BRANCHES * D), 0.01),
        "w_out": nrm((DEPTH, D, D), D ** -0.5),
        "norm_ffn": 1.0 + nrm((DEPTH, D), 0.01),
        "w_group": nrm((DEPTH, D, N_GROUPS), D ** -0.5),
        "b_group": nrm((DEPTH, N_GROUPS), 0.01),
        "w_router": nrm((DEPTH, D, N_EXPERTS), D ** -0.5),
        "b_router": nrm((DEPTH, N_EXPERTS), 0.01),
        "w_expert_gate": nrm((DEPTH, N_EXPERTS, D, EXPERT_FF), D ** -0.5),
        "w_expert_up": nrm((DEPTH, N_EXPERTS, D, EXPERT_FF), D ** -0.5),
        "w_expert_down": nrm((DEPTH, N_EXPERTS, EXPERT_FF, D), EXPERT_FF ** -0.5),
        "norm_final": 1.0 + nrm((D,), 0.01),
    }


def reference(x_prompt, x_sample, mem_prompt, cache_nsa_cmp_k, cache_nsa_cmp_v, cache_nsa_slc_k,
              cache_nsa_slc_v, cache_nsa_win_k, cache_nsa_win_v, cache_moba_k, cache_moba_v,
              cache_mem_k, cache_mem_v, page_table, norm_mix, norm_mem, w_in, w_mem_kv,
              cmp_w1_k, cmp_b1_k, cmp_w2_k, cmp_w1_v, cmp_b1_v, cmp_w2_v, w_nsa_out, w_moba_out,
              w_mem_out, w_merge_gate, b_merge_gate, w_out, norm_ffn, w_group, b_group, w_router,
              b_router, w_expert_gate, w_expert_up, w_expert_down, norm_final):
    xp, xs = x_prompt, x_sample
    p_layers, s_layers = [], []
    for l in range(DEPTH):
        lw = dict(norm_mix=norm_mix[l], norm_mem=norm_mem[l], w_in=w_in[l], w_mem_kv=w_mem_kv[l],
                  cmp_w1_k=cmp_w1_k[l], cmp_b1_k=cmp_b1_k[l], cmp_w2_k=cmp_w2_k[l],
                  cmp_w1_v=cmp_w1_v[l], cmp_b1_v=cmp_b1_v[l], cmp_w2_v=cmp_w2_v[l],
                  w_nsa_out=w_nsa_out[l], w_moba_out=w_moba_out[l], w_mem_out=w_mem_out[l],
                  w_merge_gate=w_merge_gate[l], b_merge_gate=b_merge_gate[l], w_out=w_out[l],
                  norm_ffn=norm_ffn[l], w_group=w_group[l], b_group=b_group[l],
                  w_router=w_router[l], b_router=b_router[l], w_expert_gate=w_expert_gate[l],
                  w_expert_up=w_expert_up[l], w_expert_down=w_expert_down[l])
        xp, p_new = prompt_layer(xp, mem_prompt, lw)
        xs, s_new = sample_layer(xs, cache_nsa_cmp_k[l], cache_nsa_cmp_v[l], cache_nsa_slc_k[l],
                                 cache_nsa_slc_v[l], cache_nsa_win_k[l], cache_nsa_win_v[l],
                                 cache_moba_k[l], cache_moba_v[l], cache_mem_k[l], cache_mem_v[l],
                                 page_table, lw)
        p_layers.append(p_new)
        s_layers.append(s_new)
    y_prompt = rms_norm(xp, norm_final)
    y_sample = rms_norm(xs, norm_final)
    (p_cmp_k, p_cmp_v, p_slc_k, p_slc_v, p_win_k, p_win_v, p_moba_k, p_moba_v,
     p_mem_k, p_mem_v) = [jnp.stack(z) for z in zip(*p_layers)]
    (s_cmp_k, s_cmp_v, s_slc_k, s_slc_v, s_win_k, s_win_v, s_moba_k,
     s_moba_v) = [jnp.stack(z) for z in zip(*s_layers)]
    return (y_prompt, y_sample, p_cmp_k, p_cmp_v, p_slc_k, p_slc_v, p_win_k, p_win_v,
            p_moba_k, p_moba_v, p_mem_k, p_mem_v, s_cmp_k, s_cmp_v, s_slc_k, s_slc_v,
            s_win_k, s_win_v, s_moba_k, s_moba_v)
```

```python
import functools

import numpy as np
import jax
import jax.numpy as jnp
from jax import lax
from jax.experimental import pallas as pl
from jax.experimental.pallas import tpu as pltpu

f32 = jnp.float32
bf16 = jnp.bfloat16

D_MODEL = 1024
PAGE_SIZE = 128
HEAD_DIM = 64
NSA_HEADS = 8
NSA_GROUPS = 2
NSA_REP = NSA_HEADS // NSA_GROUPS
NSA_CMP_LEN = 32
NSA_CMP_STRIDE = 16
NSA_CMP_HIDDEN = 256
NSA_SEL_BLOCK = 64
NSA_TOPN = 16
NSA_WINDOW = 512
MOBA_HEADS = 8
MOBA_BLOCK = 256
MOBA_TOPK = 3
MEM_HEADS = 4
MEM_HEAD_DIM = 128
N_BRANCHES = 3
N_GROUPS = 4
EXPERTS_PER_GROUP = 8
N_EXPERTS = N_GROUPS * EXPERTS_PER_GROUP
TOP_K_IN_GROUP = 2
EXPERT_FF = 512
ROPE_THETA = 10000.0
NORM_EPS = 1e-6

NSA_Q_W = NSA_HEADS * HEAD_DIM
NSA_KV_W = NSA_GROUPS * HEAD_DIM
NSA_GATE_W = NSA_HEADS * 3
MOBA_W = MOBA_HEADS * HEAD_DIM
MEM_W = MEM_HEADS * MEM_HEAD_DIM
PROJ_WIDTH = NSA_Q_W + NSA_GATE_W + 6 * NSA_KV_W + 3 * MOBA_W + MEM_W

LANES = 128
MASKED = -1e30
ROW_MAX_INIT = -1e20
FORCED = 1e30
MOE_ROWS = 256


def _round_up(n, m):
    return -(-n // m) * m


def _apply_act(acc, act):
    if act == "sigmoid":
        return jax.nn.sigmoid(acc)
    if act == "gelu":
        return jax.nn.gelu(acc)
    assert act is None
    return acc


def _mm_kernel(x_ref, w_ref, b_ref, o_ref, *, act):
    acc = jnp.dot(x_ref[...].astype(bf16), w_ref[...], preferred_element_type=f32)
    o_ref[...] = _apply_act(acc + b_ref[...], act).astype(o_ref.dtype)


def _norm_mm_kernel(x_ref, g_ref, w_ref, b_ref, o_ref, xn_ref, *, act):
    @pl.when(pl.program_id(1) == 0)
    def _():
        xf = x_ref[...]
        y = xf * lax.rsqrt(jnp.mean(xf * xf, axis=-1, keepdims=True) + NORM_EPS)
        xn_ref[...] = (y * g_ref[...]).astype(bf16)

    acc = jnp.dot(xn_ref[...], w_ref[...], preferred_element_type=f32)
    o_ref[...] = _apply_act(acc + b_ref[...], act).astype(o_ref.dtype)


def _matmul(x, w, *, bias=None, gain=None, act=None, out_dtype=f32, tm=512, tn=512, name="matmul"):
    M, K = x.shape
    N = w.shape[1]
    tm = min(tm, M)
    tn = max(t for t in range(LANES, min(tn, N) + 1, LANES) if N % t == 0)
    assert M % tm == 0 and N % LANES == 0, (M, N, tm, tn)
    if bias is None:
        bias = jnp.zeros((N,), f32)
    bias = bias.reshape(1, N).astype(f32)
    grid = (M // tm, N // tn)
    x_spec = pl.BlockSpec((tm, K), lambda i, j: (i, 0))
    w_spec = pl.BlockSpec((K, tn), lambda i, j: (0, j))
    b_spec = pl.BlockSpec((1, tn), lambda i, j: (0, j))
    o_spec = pl.BlockSpec((tm, tn), lambda i, j: (i, j))
    out_shape = jax.ShapeDtypeStruct((M, N), out_dtype)
    if gain is None:
        return pl.pallas_call(
            functools.partial(_mm_kernel, act=act), grid=grid,
            in_specs=[x_spec, w_spec, b_spec], out_specs=o_spec, out_shape=out_shape,
            compiler_params=pltpu.CompilerParams(dimension_semantics=("parallel", "arbitrary")),
            name=name)(x, w, bias)
    return pl.pallas_call(
        functools.partial(_norm_mm_kernel, act=act), grid=grid,
        in_specs=[x_spec, pl.BlockSpec((1, K), lambda i, j: (0, 0)), w_spec, b_spec],
        out_specs=o_spec, out_shape=out_shape,
        scratch_shapes=[pltpu.VMEM((tm, K), bf16)],
        compiler_params=pltpu.CompilerParams(dimension_semantics=("parallel", "arbitrary")),
        name=name)(x, gain.reshape(1, K).astype(f32), w, bias)


def _split_hi_lo(a):
    hi = a.astype(bf16)
    lo = (a - hi.astype(f32)).astype(bf16)
    return hi, lo


def _dot_f32(a, b):
    a_hi, a_lo = _split_hi_lo(a)
    b_hi, b_lo = _split_hi_lo(b)
    return (jnp.dot(a_hi, b_hi, preferred_element_type=f32)
            + jnp.dot(a_lo, b_hi, preferred_element_type=f32)
            + jnp.dot(a_hi, b_lo, preferred_element_type=f32))


def _flash_tile(q, kt, vt, mask, m, l, acc, rep):
    tq, tk = mask.shape
    s = jnp.dot(q, kt, preferred_element_type=f32).reshape(rep, tq, tk)
    s = jnp.where(mask[None], s, MASKED)
    m_new = jnp.maximum(m, jnp.max(s, axis=-1, keepdims=True))
    p = jnp.exp(s - m_new)
    alpha = jnp.exp(m - m_new)
    l = alpha * l + jnp.sum(p, axis=-1, keepdims=True)
    pv = jnp.dot(p.reshape(rep * tq, tk).astype(bf16), vt, preferred_element_type=f32)
    acc = alpha.reshape(rep * tq, 1) * acc + pv
    return m_new, l, acc


def _flash_init(rep, tq, d):
    return (jnp.full((rep, tq, 1), ROW_MAX_INIT, f32), jnp.zeros((rep, tq, 1), f32),
            jnp.zeros((rep * tq, d), f32))


def _flash_finish(l, acc):
    l = l.reshape(acc.shape[0], 1)
    return acc / jnp.where(l > 0, l, 1.0)


def _topk_mask(score, k, col_index, n_real):
    rank = jnp.zeros(score.shape, jnp.int32)
    for i in range(n_real):
        si = score[:, i:i + 1]
        ahead = (si > score) | ((si == score) & (col_index > i))
        rank = rank + jnp.where(ahead, 1, 0)
    return rank < k


def _mask_to_bf16(mask):
    return jnp.where(mask, 1.0, 0.0).astype(bf16)


def _nsa_prompt_kernel(q_ref, g_ref, ckT_ref, cv_ref, kT_ref, v_ref, wkT_ref, wv_ref,
                       imp_ref, esel_ref, o_ref, selexp_ref, *, tq, tk, tkw, n_sel):
    rep, d = NSA_REP, HEAD_DIM
    s0 = pl.program_id(1) * tq
    qpos = s0 + lax.broadcasted_iota(jnp.int32, (tq, 1), 0)
    n_cmp = ckT_ref.shape[-1]
    cmp_end = lax.broadcasted_iota(jnp.int32, (1, n_cmp), 1) * NSA_CMP_STRIDE + (NSA_CMP_LEN - 1)
    cmp_mask = cmp_end <= qpos
    jsel = lax.broadcasted_iota(jnp.int32, (1, LANES), 1)
    cur = jnp.right_shift(qpos, NSA_SEL_BLOCK.bit_length() - 1)
    visible = jsel <= cur
    forced = (jsel == 0) | (jsel == cur) | (jsel == cur - 1)

    for g in range(NSA_GROUPS):
        q = q_ref[0, g].reshape(rep * tq, d)
        s = jnp.dot(q, ckT_ref[0, g], preferred_element_type=f32).reshape(rep, tq, n_cmp)
        s = jnp.where(cmp_mask[None], s, MASKED)
        m = jnp.maximum(jnp.max(s, axis=-1, keepdims=True), ROW_MAX_INIT)
        e = jnp.exp(s - m)
        den = jnp.sum(e, axis=-1, keepdims=True)
        p_c = e / jnp.where(den > 0, den, 1.0)
        o_c = jnp.dot(p_c.reshape(rep * tq, n_cmp).astype(bf16), cv_ref[0, g],
                      preferred_element_type=f32)
        p_hi, p_lo = _split_hi_lo(jnp.sum(p_c, axis=0))
        imp = (jnp.dot(p_hi, imp_ref[...], preferred_element_type=f32)
               + jnp.dot(p_lo, imp_ref[...], preferred_element_type=f32))
        score = jnp.where(visible, jnp.where(forced, FORCED, imp), -FORCED)
        sel = _topk_mask(score, NSA_TOPN, jsel, n_sel) & visible
        selexp_ref[...] = jnp.dot(_mask_to_bf16(sel), esel_ref[...], preferred_element_type=f32)

        def slc_body(t, carry):
            k0 = pl.multiple_of(t * tk, tk)
            kpos = k0 + lax.broadcasted_iota(jnp.int32, (1, tk), 1)
            mask = (selexp_ref[:, pl.ds(k0, tk)] > 0.5) & (kpos <= qpos)
            return _flash_tile(q, kT_ref[0, g, :, pl.ds(k0, tk)], v_ref[0, g, pl.ds(k0, tk), :],
                               mask, *carry, rep)

        _, l, acc = lax.fori_loop(0, (s0 + tq + tk - 1) // tk, slc_body, _flash_init(rep, tq, d))
        o_s = _flash_finish(l, acc)

        def win_body(t, carry):
            k0 = pl.multiple_of(t * tkw, tkw)
            kpos = k0 + lax.broadcasted_iota(jnp.int32, (1, tkw), 1)
            mask = (kpos <= qpos) & (kpos > qpos - NSA_WINDOW)
            return _flash_tile(q, wkT_ref[0, g, :, pl.ds(k0, tkw)], wv_ref[0, g, pl.ds(k0, tkw), :],
                               mask, *carry, rep)

        w_lo = jnp.maximum(s0 - NSA_WINDOW, 0) // tkw
        _, l, acc = lax.fori_loop(w_lo, (s0 + tq) // tkw, win_body, _flash_init(rep, tq, d))
        o_w = _flash_finish(l, acc)

        for r in range(rep):
            h = g * rep + r
            rows = slice(r * tq, (r + 1) * tq)
            o_h = (g_ref[0, :, 3 * h:3 * h + 1] * o_c[rows]
                   + g_ref[0, :, 3 * h + 1:3 * h + 2] * o_s[rows]
                   + g_ref[0, :, 3 * h + 2:3 * h + 3] * o_w[rows])
            o_ref[0, :, h * d:(h + 1) * d] = o_h


def _nsa_static_maps(n_cmp_pad, n_sel):
    ratio_c = NSA_CMP_LEN // NSA_CMP_STRIDE
    ratio_s = NSA_SEL_BLOCK // NSA_CMP_STRIDE
    nc = n_cmp_pad - ratio_c + 1
    mat = np.zeros((n_cmp_pad, n_sel), np.float32)
    j = np.arange(n_sel)
    for mm in range(ratio_s):
        for n in range(ratio_c):
            i = ratio_s * j + mm - n
            ok = (i >= 0) & (i < nc)
            np.add.at(mat, (i[ok], j[ok]), 1.0)
    return mat


def _block_expand(n_blocks, block, total):
    k = np.arange(total)
    return (k[None, :] // block == np.arange(n_blocks)[:, None]).astype(np.float32)


def _nsa_prompt(q, gates, c_k, c_v, k_slc, v_slc, k_win, v_win, *, tq=128, tk=256, tkw=128):
    B, S = q.shape[:2]
    G, R, d = NSA_GROUPS, NSA_REP, HEAD_DIM
    n_cmp = S // NSA_CMP_STRIDE
    n_sel = max(S // NSA_SEL_BLOCK, NSA_TOPN)
    assert S % tq == 0 and S % tk == 0 and tq % tkw == 0 and NSA_WINDOW % tkw == 0
    assert n_sel * NSA_SEL_BLOCK == S and c_k.shape[1] == n_cmp - 1
    qg = (q * (d ** -0.5)).astype(bf16).reshape(B, S, G, R, d).transpose(0, 2, 3, 1, 4)
    pad_c = ((0, 0), (0, 1), (0, 0), (0, 0))
    ckT = jnp.pad(c_k, pad_c).astype(bf16).transpose(0, 2, 3, 1)
    cv = jnp.pad(c_v, pad_c).astype(bf16).transpose(0, 2, 1, 3)
    kT = lambda a: a.astype(bf16).transpose(0, 2, 3, 1)
    vv = lambda a: a.astype(bf16).transpose(0, 2, 1, 3)
    assert n_sel <= LANES
    imp = jnp.asarray(np.pad(_nsa_static_maps(n_cmp, n_sel), ((0, 0), (0, LANES - n_sel))), bf16)
    esel = jnp.asarray(_block_expand(LANES, NSA_SEL_BLOCK, S), bf16)
    per_b = lambda shape: pl.BlockSpec((1,) + shape, lambda b, i: (b,) + (0,) * len(shape))
    full = lambda shape: pl.BlockSpec(shape, lambda b, i: (0,) * len(shape))
    return pl.pallas_call(
        functools.partial(_nsa_prompt_kernel, tq=tq, tk=tk, tkw=tkw, n_sel=n_sel),
        grid=(B, S // tq),
        in_specs=[pl.BlockSpec((1, G, R, tq, d), lambda b, i: (b, 0, 0, i, 0)),
                  pl.BlockSpec((1, tq, NSA_GATE_W), lambda b, i: (b, i, 0)),
                  per_b((G, d, n_cmp)), per_b((G, n_cmp, d)),
                  per_b((G, d, S)), per_b((G, S, d)), per_b((G, d, S)), per_b((G, S, d)),
                  full((n_cmp, LANES)), full((LANES, S))],
        out_specs=pl.BlockSpec((1, tq, NSA_Q_W), lambda b, i: (b, i, 0)),
        out_shape=jax.ShapeDtypeStruct((B, S, NSA_Q_W), f32),
        scratch_shapes=[pltpu.VMEM((tq, S), f32)],
        compiler_params=pltpu.CompilerParams(dimension_semantics=("parallel", "arbitrary")),
        name="nsa_prompt")(qg, gates, ckT, cv, kT(k_slc), vv(v_slc), kT(k_win), vv(v_win), imp, esel)


def _moba_prompt_kernel(q_ref, kT_ref, v_ref, ablk_ref, esel_ref, o_ref, meansT_ref, selexp_ref, *, tq, n_blk):
    d, tk = HEAD_DIM, MOBA_BLOCK
    n_heads = q_ref.shape[1]
    qi = pl.program_id(1)
    s0 = qi * tq
    cur = s0 // MOBA_BLOCK
    qpos = s0 + lax.broadcasted_iota(jnp.int32, (tq, 1), 0)
    jblk = lax.broadcasted_iota(jnp.int32, (1, LANES), 1)
    past = jblk < cur

    @pl.when(qi == 0)
    def _():
        for h in range(n_heads):
            k_hi, k_lo = _split_hi_lo(kT_ref[0, h])
            meansT_ref[h] = (jnp.dot(k_hi, ablk_ref[...], preferred_element_type=f32)
                             + jnp.dot(k_lo, ablk_ref[...], preferred_element_type=f32))

    for h in range(n_heads):
        qf = q_ref[0, h]
        gate = _dot_f32(qf, meansT_ref[h])
        score = jnp.where(past, gate, -FORCED)
        sel = (_topk_mask(score, MOBA_TOPK, jblk, n_blk) & past) | (jblk == cur)
        selexp_ref[...] = jnp.dot(_mask_to_bf16(sel), esel_ref[...], preferred_element_type=f32)
        q = (qf * (d ** -0.5)).astype(bf16)

        def body(t, carry):
            k0 = pl.multiple_of(t * tk, tk)
            kpos = k0 + lax.broadcasted_iota(jnp.int32, (1, tk), 1)
            mask = (selexp_ref[:, pl.ds(k0, tk)] > 0.5) & (kpos <= qpos)
            return _flash_tile(q, kT_ref[0, h, :, pl.ds(k0, tk)].astype(bf16),
                               v_ref[0, h, pl.ds(k0, tk), :], mask, *carry, 1)

        _, l, acc = lax.fori_loop(0, cur + 1, body, _flash_init(1, tq, d))
        o_ref[0, :, h * d:(h + 1) * d] = _flash_finish(l, acc)


def _moba_prompt(q, k, v, *, tq=256):
    B, S, H, d = q.shape
    n_blk = S // MOBA_BLOCK
    assert S % MOBA_BLOCK == 0 and MOBA_BLOCK % tq == 0 and n_blk >= MOBA_TOPK
    qh = q.transpose(0, 2, 1, 3)
    kT = k.transpose(0, 2, 3, 1)
    vh = v.astype(bf16).transpose(0, 2, 1, 3)
    assert n_blk <= LANES
    esel_np = _block_expand(LANES, MOBA_BLOCK, S)
    ablk = jnp.asarray(esel_np.T / MOBA_BLOCK, bf16)
    esel = jnp.asarray(esel_np, bf16)
    per_b = lambda shape: pl.BlockSpec((1,) + shape, lambda b, i: (b,) + (0,) * len(shape))
    full = lambda shape: pl.BlockSpec(shape, lambda b, i: (0,) * len(shape))
    return pl.pallas_call(
        functools.partial(_moba_prompt_kernel, tq=tq, n_blk=n_blk),
        grid=(B, S // tq),
        in_specs=[pl.BlockSpec((1, H, tq, d), lambda b, i: (b, 0, i, 0)),
                  per_b((H, d, S)), per_b((H, S, d)), full((S, LANES)), full((LANES, S))],
        out_specs=pl.BlockSpec((1, tq, H * d), lambda b, i: (b, i, 0)),
        out_shape=jax.ShapeDtypeStruct((B, S, H * d), f32),
        scratch_shapes=[pltpu.VMEM((H, d, LANES), f32), pltpu.VMEM((tq, S), f32)],
        compiler_params=pltpu.CompilerParams(dimension_semantics=("parallel", "arbitrary")),
        name="moba_prompt")(qh, kT, vh, ablk, esel)


def _mem_attn_kernel(q_ref, kT_ref, v_ref, o_ref):
    d = MEM_HEAD_DIM
    for h in range(MEM_HEADS):
        q = q_ref[0, :, h * d:(h + 1) * d].astype(bf16)
        s = jnp.dot(q, kT_ref[0, h], preferred_element_type=f32) * (d ** -0.5)
        e = jnp.exp(s - jnp.max(s, axis=-1, keepdims=True))
        p = e / jnp.sum(e, axis=-1, keepdims=True)
        o_ref[0, :, h * d:(h + 1) * d] = jnp.dot(p.astype(bf16), v_ref[0, h], preferred_element_type=f32)


def _mem_attn(q, mk, mv, *, tq=512):
    B, T, W = q.shape
    M = mk.shape[1]
    tq = min(tq, T)
    assert T % tq == 0
    kT = mk.astype(bf16).transpose(0, 2, 3, 1)
    vh = mv.astype(bf16).transpose(0, 2, 1, 3)
    return pl.pallas_call(
        _mem_attn_kernel, grid=(B, T // tq),
        in_specs=[pl.BlockSpec((1, tq, W), lambda b, i: (b, i, 0)),
                  pl.BlockSpec((1, MEM_HEADS, MEM_HEAD_DIM, M), lambda b, i: (b, 0, 0, 0)),
                  pl.BlockSpec((1, MEM_HEADS, M, MEM_HEAD_DIM), lambda b, i: (b, 0, 0, 0))],
        out_specs=pl.BlockSpec((1, tq, W), lambda b, i: (b, i, 0)),
        out_shape=jax.ShapeDtypeStruct((B, T, W), f32),
        compiler_params=pltpu.CompilerParams(dimension_semantics=("parallel", "arbitrary")),
        name="mem_attn")(q, kT, vh)


def _router_kernel(x_ref, g_ref, w_ref, b_ref, h_ref, o_ref):
    xf = x_ref[...]
    y = xf * lax.rsqrt(jnp.mean(xf * xf, axis=-1, keepdims=True) + NORM_EPS) * g_ref[...]
    h_ref[...] = y.astype(bf16)
    o_ref[...] = _dot_f32(y, w_ref[...]) + b_ref[...]


def _router(x, gain, w, b, *, tm=512):
    M, K = x.shape
    N = w.shape[1]
    tm = min(tm, M)
    assert M % tm == 0
    return pl.pallas_call(
        _router_kernel, grid=(M // tm,),
        in_specs=[pl.BlockSpec((tm, K), lambda i: (i, 0)), pl.BlockSpec((1, K), lambda i: (0, 0)),
                  pl.BlockSpec((K, N), lambda i: (0, 0)), pl.BlockSpec((1, N), lambda i: (0, 0))],
        out_specs=[pl.BlockSpec((tm, K), lambda i: (i, 0)), pl.BlockSpec((tm, N), lambda i: (i, 0))],
        out_shape=[jax.ShapeDtypeStruct((M, K), bf16), jax.ShapeDtypeStruct((M, N), f32)],
        compiler_params=pltpu.CompilerParams(dimension_semantics=("parallel",)),
        name="router")(x, gain.reshape(1, K), w, b.reshape(1, N))


def _expert_kernel(be_ref, x_ref, wg_ref, wu_ref, wd_ref, o_ref):
    x = x_ref[...]
    a = jnp.dot(x, wg_ref[0], preferred_element_type=f32)
    u = jnp.dot(x, wu_ref[0], preferred_element_type=f32)
    mid = (a * jax.nn.sigmoid(a)) * u
    o_ref[...] = jnp.dot(mid.astype(bf16), wd_ref[0], preferred_element_type=f32)


def _expert_ffn(blk_expert, xs, w_gate, w_up, w_down):
    cap, D = xs.shape
    F = w_gate.shape[-1]
    return pl.pallas_call(
        _expert_kernel,
        grid_spec=pltpu.PrefetchScalarGridSpec(
            num_scalar_prefetch=1, grid=(cap // MOE_ROWS,),
            in_specs=[pl.BlockSpec((MOE_ROWS, D), lambda i, be: (i, 0)),
                      pl.BlockSpec((1, D, F), lambda i, be: (be[i], 0, 0)),
                      pl.BlockSpec((1, D, F), lambda i, be: (be[i], 0, 0)),
                      pl.BlockSpec((1, F, D), lambda i, be: (be[i], 0, 0))],
            out_specs=pl.BlockSpec((MOE_ROWS, D), lambda i, be: (i, 0))),
        out_shape=jax.ShapeDtypeStruct((cap, D), f32),
        compiler_params=pltpu.CompilerParams(dimension_semantics=("arbitrary",)),
        name="expert_ffn")(blk_expert, xs, w_gate, w_up, w_down)


def _moe(x, lw):
    n_tok, D = x.shape
    n_pad = _round_up(n_tok, 256)
    xp = jnp.pad(x, ((0, n_pad - n_tok), (0, 0)))
    n_route = N_GROUPS + N_EXPERTS
    w_route = jnp.pad(jnp.concatenate([lw["w_group"], lw["w_router"]], axis=1), ((0, 0), (0, LANES - n_route)))
    b_route = jnp.pad(jnp.concatenate([lw["b_group"], lw["b_router"]]), (0, LANES - n_route))
    h, logits = _router(xp, lw["norm_ffn"], w_route, b_route, tm=256)
    logits = logits[:n_tok]
    g_logits = logits[:, :N_GROUPS]
    g_prob = jax.nn.softmax(g_logits, axis=-1)
    grp = jnp.argmax(g_logits, axis=-1).astype(jnp.int32)
    e_logits = logits[:, N_GROUPS:n_route].reshape(-1, N_GROUPS, EXPERTS_PER_GROUP)
    e_in = jnp.take_along_axis(e_logits, grp[:, None, None], axis=1)[:, 0]
    top_p, top_i = lax.top_k(jax.nn.softmax(e_in, axis=-1), TOP_K_IN_GROUP)
    weight = top_p / jnp.sum(top_p, axis=-1, keepdims=True) * jnp.take_along_axis(g_prob, grp[:, None], axis=1)
    expert = grp[:, None] * EXPERTS_PER_GROUP + top_i.astype(jnp.int32)
    n_assign = n_tok * TOP_K_IN_GROUP
    n_blocks = -(-n_assign // MOE_ROWS) + N_EXPERTS
    cap = n_blocks * MOE_ROWS
    flat_e = expert.reshape(-1)
    flat_t = jnp.repeat(jnp.arange(n_tok, dtype=jnp.int32), TOP_K_IN_GROUP)
    onehot = (flat_e[:, None] == jnp.arange(N_EXPERTS, dtype=jnp.int32)[None, :]).astype(jnp.int32)
    csum = jnp.cumsum(onehot, axis=0)
    rank = jnp.take_along_axis(csum, flat_e[:, None], axis=1)[:, 0] - 1
    counts = csum[-1]
    padded = (counts + MOE_ROWS - 1) // MOE_ROWS * MOE_ROWS
    pad_end = jnp.cumsum(padded)
    dest = (pad_end - padded)[flat_e] + rank
    slot_tok = jnp.full((cap,), n_tok, jnp.int32).at[dest].set(flat_t)
    blk_expert = jnp.minimum(
        jnp.searchsorted(pad_end, jnp.arange(n_blocks, dtype=jnp.int32) * MOE_ROWS, side="right"),
        N_EXPERTS - 1).astype(jnp.int32)
    xs = h[slot_tok]
    ys = _expert_ffn(blk_expert, xs, lw["w_expert_gate"].astype(bf16), lw["w_expert_up"].astype(bf16),
                     lw["w_expert_down"].astype(bf16))
    picked = ys[dest].reshape(n_tok, TOP_K_IN_GROUP, D)
    return jnp.sum(picked * weight[:, :, None], axis=1)


def _rms_norm(x, gain):
    y = x * lax.rsqrt(jnp.mean(x * x, axis=-1, keepdims=True) + NORM_EPS)
    return y * gain


def _rotary(x, pos):
    half = x.shape[-1] // 2
    inv_freq = ROPE_THETA ** (-jnp.arange(half, dtype=f32) / half)
    ang = pos.astype(f32)[:, None] * inv_freq[None, :]
    cos = jnp.cos(ang)[None, :, None, :]
    sin = jnp.sin(ang)[None, :, None, :]
    x1, x2 = x[..., :half], x[..., half:]
    return jnp.concatenate([x1 * cos - x2 * sin, x2 * cos + x1 * sin], axis=-1)


_PROJ_PAD = _round_up(PROJ_WIDTH, LANES)


def _proj_weight(w_in):
    w = jnp.concatenate([w_in[:, :NSA_Q_W], w_in[:, NSA_Q_W + NSA_GATE_W:],
                         w_in[:, NSA_Q_W:NSA_Q_W + NSA_GATE_W]], axis=1)
    return jnp.pad(w, ((0, 0), (0, _PROJ_PAD - PROJ_WIDTH))).astype(bf16)


def _mixer_inputs(x, pos, lw):
    B, T, D = x.shape
    proj = _matmul(x.reshape(B * T, D), _proj_weight(lw["w_in"]), gain=lw["norm_mix"], name="proj_in")
    proj = proj.reshape(B, T, _PROJ_PAD)
    c = [0]

    def take(width, n, d):
        a = proj[:, :, c[0]:c[0] + width].reshape(B, T, n, d)
        c[0] += width
        return a

    nsa_q = _rotary(take(NSA_Q_W, NSA_HEADS, HEAD_DIM), pos)
    k_cmp = _rotary(take(NSA_KV_W, NSA_GROUPS, HEAD_DIM), pos)
    v_cmp = take(NSA_KV_W, NSA_GROUPS, HEAD_DIM)
    k_slc = _rotary(take(NSA_KV_W, NSA_GROUPS, HEAD_DIM), pos)
    v_slc = take(NSA_KV_W, NSA_GROUPS, HEAD_DIM)
    k_win = _rotary(take(NSA_KV_W, NSA_GROUPS, HEAD_DIM), pos)
    v_win = take(NSA_KV_W, NSA_GROUPS, HEAD_DIM)
    moba_q = _rotary(take(MOBA_W, MOBA_HEADS, HEAD_DIM), pos)
    moba_k = _rotary(take(MOBA_W, MOBA_HEADS, HEAD_DIM), pos)
    moba_v = take(MOBA_W, MOBA_HEADS, HEAD_DIM)
    mem_q = proj[:, :, c[0]:c[0] + MEM_W]
    c[0] += MEM_W
    nsa_g = jax.nn.sigmoid(proj[:, :, c[0]:c[0] + NSA_GATE_W])
    return nsa_q, nsa_g, k_cmp, v_cmp, k_slc, v_slc, k_win, v_win, moba_q, moba_k, moba_v, mem_q


def _compress(rows, w1, b1, w2):
    B, L, G, dh = rows.shape
    n_chunks = L // NSA_CMP_STRIDE
    cw = NSA_CMP_STRIDE * dh
    chunks = rows.reshape(B, n_chunks, NSA_CMP_STRIDE, G, dh).transpose(0, 1, 3, 2, 4)
    chunks = chunks.reshape(B * n_chunks * G, cw)
    w1cat = jnp.concatenate([w1[:cw], w1[cw:]], axis=1).astype(bf16)
    part = _matmul(chunks, w1cat, name="cmp_hidden").reshape(B, n_chunks, G, 2, NSA_CMP_HIDDEN)
    nxt = jnp.concatenate([part[:, 1:, :, 1], jnp.zeros_like(part[:, :1, :, 1])], axis=1)
    hid = b1 + part[:, :, :, 0] + nxt
    w2p = jnp.pad(w2, ((0, 0), (0, LANES - dh))).astype(bf16)
    out = _matmul(jax.nn.gelu(hid).reshape(B * n_chunks * G, NSA_CMP_HIDDEN), w2p, name="cmp_out")
    return out[:, :dh].reshape(B, n_chunks, G, dh)[:, :n_chunks - 1]


def _merge_and_ffn(x, o_nsa, o_moba, o_mem, lw):
    B, T, D = x.shape
    x2 = x.reshape(B * T, D)
    gates = _matmul(x2, lw["w_merge_gate"].astype(bf16), gain=lw["norm_mix"], bias=lw["b_merge_gate"],
                    act="sigmoid", name="merge_gate")
    mixed = (gates[:, :D] * _matmul(o_nsa.reshape(B * T, -1), lw["w_nsa_out"].astype(bf16), name="nsa_out")
             + gates[:, D:2 * D] * _matmul(o_moba.reshape(B * T, -1), lw["w_moba_out"].astype(bf16), name="moba_out")
             + gates[:, 2 * D:] * _matmul(o_mem.reshape(B * T, -1), lw["w_mem_out"].astype(bf16), name="mem_out"))
    return x2 + _matmul(mixed, lw["w_out"].astype(bf16), name="w_out")


def _prompt_layer(x, mem, lw):
    B, S, D = x.shape
    pos = jnp.arange(S, dtype=jnp.int32)
    (nsa_q, nsa_g, k_cmp, v_cmp, k_slc, v_slc, k_win, v_win,
     moba_q, moba_k, moba_v, mem_q) = _mixer_inputs(x, pos, lw)
    Bm, M, _ = mem.shape
    kv = _matmul(mem.reshape(Bm * M, D), lw["w_mem_kv"].astype(bf16), gain=lw["norm_mem"], name="mem_kv")
    kv = kv.reshape(Bm, M, 2, MEM_HEADS, MEM_HEAD_DIM)
    mem_k, mem_v = kv[:, :, 0], kv[:, :, 1]
    c_k = _compress(k_cmp, lw["cmp_w1_k"], lw["cmp_b1_k"], lw["cmp_w2_k"])
    c_v = _compress(v_cmp, lw["cmp_w1_v"], lw["cmp_b1_v"], lw["cmp_w2_v"])
    o_nsa = _nsa_prompt(nsa_q, nsa_g, c_k, c_v, k_slc, v_slc, k_win, v_win)
    o_moba = _moba_prompt(moba_q, moba_k, moba_v)
    o_mem = _mem_attn(mem_q, mem_k, mem_v)
    x_mid = _merge_and_ffn(x, o_nsa, o_moba, o_mem, lw)
    wb = min(NSA_WINDOW, S)
    state = (k_cmp, v_cmp, k_slc, v_slc, k_win[:, S - wb:], v_win[:, S - wb:], moba_k, moba_v, mem_k, mem_v)
    return x_mid, state


def _masked_softmax(scores, mask):
    s = jnp.where(mask, scores, -jnp.inf)
    m = jnp.max(s, axis=-1, keepdims=True)
    e = jnp.exp(s - jnp.where(jnp.isfinite(m), m, 0.0))
    dsum = jnp.sum(e, axis=-1, keepdims=True)
    return e / jnp.where(dsum > 0, dsum, 1.0)


def _contiguous_fetch(rows):
    total = rows.shape[1]

    def fetch(pos, hidx):
        return jax.vmap(lambda r, p, g: r[p, g])(rows, jnp.clip(pos, 0, total - 1), hidx)
    return fetch


def _paged_fetch(pool, page_table, new_rows):
    n_seq, n_pages = page_table.shape
    past_len = n_pages * PAGE_SIZE
    n_new = new_rows.shape[1]
    total = past_len + n_new

    def fetch(pos, hidx):
        pos = jnp.clip(pos, 0, total - 1)
        in_past = pos < past_len
        pp = jnp.minimum(pos, past_len - 1)
        logical = (pp // PAGE_SIZE).reshape(n_seq, -1)
        phys = jnp.take_along_axis(page_table, logical, axis=1).reshape(pos.shape)
        from_pool = pool[phys, pp % PAGE_SIZE, hidx]
        npos = jnp.clip(pos - past_len, 0, n_new - 1)
        from_new = jax.vmap(lambda r, p, g: r[p, g])(new_rows, npos, hidx)
        return jnp.where(in_past[..., None], from_pool, from_new)
    return fetch


def _sample_nsa(q, gates, q_pos, cmp_k, cmp_v, total, fetch_k, fetch_v, win_k, win_v, win_pos):
    B, T = q.shape[:2]
    scale = HEAD_DIM ** -0.5
    nc = cmp_k.shape[1]
    nsb = max(-(-total // NSA_SEL_BLOCK), NSA_TOPN)
    imp_map = jnp.asarray(_nsa_static_maps(nc + 1, nsb)[:nc])
    cmp_end = jnp.arange(nc, dtype=jnp.int32) * NSA_CMP_STRIDE + NSA_CMP_LEN - 1
    qg = q.reshape(B, T, NSA_GROUPS, NSA_REP, HEAD_DIM)
    qp = q_pos[:, None]
    s = jnp.einsum("btgrd,bcgd->btgrc", qg, cmp_k, preferred_element_type=f32) * scale
    p_c = _masked_softmax(s, (cmp_end[None, :] <= qp)[None, :, None, None, :])
    o_c = jnp.einsum("btgrc,bcgd->btgrd", p_c, cmp_v)
    imp = jnp.einsum("btgc,cj->btgj", p_c.sum(axis=3), imp_map, precision=lax.Precision.HIGHEST)
    j = jnp.arange(nsb)[None, :]
    cur = (q_pos // NSA_SEL_BLOCK)[:, None]
    visible = (j <= cur)[None, :, None, :]
    forced = ((j == 0) | (j == cur) | (j == cur - 1))[None, :, None, :]
    score = jnp.where(visible, jnp.where(forced, jnp.inf, imp), -jnp.inf)
    top_s, top_j = lax.top_k(score, NSA_TOPN)
    kpos = (top_j[..., None] * NSA_SEL_BLOCK + jnp.arange(NSA_SEL_BLOCK)).reshape(B, T, NSA_GROUPS, -1)
    kmask = jnp.repeat(top_s > -jnp.inf, NSA_SEL_BLOCK, axis=-1) & (kpos <= q_pos[None, :, None, None])
    gidx = jnp.broadcast_to(jnp.arange(NSA_GROUPS)[None, None, :, None], kpos.shape)
    ks = fetch_k(kpos, gidx)
    vs = fetch_v(kpos, gidx)
    s = jnp.einsum("btgrd,btgkd->btgrk", qg, ks, preferred_element_type=f32) * scale
    p_s = _masked_softmax(s, kmask[:, :, :, None, :])
    o_s = jnp.einsum("btgrk,btgkd->btgrd", p_s, vs)
    kp = win_pos[None, :]
    wmask = ((kp <= qp) & (kp > qp - NSA_WINDOW) & (kp >= 0))[None, :, None, None, :]
    s = jnp.einsum("btgrd,bkgd->btgrk", qg, win_k, preferred_element_type=f32) * scale
    p_w = _masked_softmax(s, wmask)
    o_w = jnp.einsum("btgrk,bkgd->btgrd", p_w, win_v)
    g = gates.reshape(B, T, NSA_GROUPS, NSA_REP, 3)
    o = g[..., 0:1] * o_c + g[..., 1:2] * o_s + g[..., 2:3] * o_w
    return o.reshape(B, T, NSA_Q_W)


def _sample_moba(q, q_pos, k_all, fetch_k, fetch_v):
    B, T, H, dh = q.shape
    L = k_all.shape[1]
    nb = L // MOBA_BLOCK
    means = k_all[:, : nb * MOBA_BLOCK].reshape(B, nb, MOBA_BLOCK, H, dh).mean(axis=2)
    gate = jnp.einsum("bthd,bjhd->bthj", q, means, preferred_element_type=f32, precision=lax.Precision.HIGHEST)
    cur = q_pos // MOBA_BLOCK
    past = (jnp.arange(nb)[None, :] < cur[:, None])[None, :, None, :]
    top_s, top_j = lax.top_k(jnp.where(past, gate, -jnp.inf), MOBA_TOPK)
    blocks = jnp.concatenate([top_j, jnp.broadcast_to(cur[None, :, None, None], (B, T, H, 1))], axis=-1)
    bvalid = jnp.concatenate([top_s > -jnp.inf, jnp.ones((B, T, H, 1), bool)], axis=-1)
    kpos = (blocks[..., None] * MOBA_BLOCK + jnp.arange(MOBA_BLOCK)).reshape(B, T, H, -1)
    kmask = jnp.repeat(bvalid, MOBA_BLOCK, axis=-1) & (kpos <= q_pos[None, :, None, None])
    hidx = jnp.broadcast_to(jnp.arange(H)[None, None, :, None], kpos.shape)
    ks = fetch_k(kpos, hidx)
    vs = fetch_v(kpos, hidx)
    s = jnp.einsum("bthd,bthkd->bthk", q, ks, preferred_element_type=f32) * (dh ** -0.5)
    p = _masked_softmax(s, kmask)
    o = jnp.einsum("bthk,bthkd->bthd", p, vs)
    return o.reshape(B, T, H * dh)


def _sample_layer(x, c_cmp_k, c_cmp_v, c_slc_k, c_slc_v, c_win_k, c_win_v,
                  c_moba_k, c_moba_v, c_mem_k, c_mem_v, page_table, lw):
    B, T, _ = x.shape
    past_len = page_table.shape[1] * PAGE_SIZE
    total = past_len + T
    pos = past_len + jnp.arange(T, dtype=jnp.int32)
    (nsa_q, nsa_g, k_cmp, v_cmp, k_slc, v_slc, k_win, v_win,
     moba_q, moba_k, moba_v, mem_q) = _mixer_inputs(x, pos, lw)
    past = lambda pool: pool[page_table].reshape(B, past_len, *pool.shape[2:])
    assert T < NSA_CMP_STRIDE and past_len % NSA_CMP_STRIDE == 0
    c_k = _compress(past(c_cmp_k), lw["cmp_w1_k"], lw["cmp_b1_k"], lw["cmp_w2_k"])
    c_v = _compress(past(c_cmp_v), lw["cmp_w1_v"], lw["cmp_b1_v"], lw["cmp_w2_v"])
    moba_k_all = jnp.concatenate([past(c_moba_k), moba_k], axis=1)
    wb = c_win_k.shape[1]
    wk_all = jnp.concatenate([c_win_k, k_win], axis=1)
    wv_all = jnp.concatenate([c_win_v, v_win], axis=1)
    win_pos = past_len - wb + jnp.arange(wb + T, dtype=jnp.int32)
    o_nsa = _sample_nsa(nsa_q, nsa_g, pos, c_k, c_v, total,
                        _paged_fetch(c_slc_k, page_table, k_slc), _paged_fetch(c_slc_v, page_table, v_slc),
                        wk_all, wv_all, win_pos)
    o_moba = _sample_moba(moba_q, pos, moba_k_all, _contiguous_fetch(moba_k_all),
                          _paged_fetch(c_moba_v, page_table, moba_v))
    mq = mem_q.reshape(B, T, MEM_HEADS, MEM_HEAD_DIM)
    s = jnp.einsum("bthd,bmhd->bthm", mq, c_mem_k, preferred_element_type=f32) * (MEM_HEAD_DIM ** -0.5)
    o_mem = jnp.einsum("bthm,bmhd->bthd", jax.nn.softmax(s, axis=-1), c_mem_v).reshape(B, T, MEM_W)
    x_mid = _merge_and_ffn(x, o_nsa, o_moba, o_mem, lw)
    state = (k_cmp, v_cmp, k_slc, v_slc, wk_all[:, -wb:], wv_all[:, -wb:], moba_k, moba_v)
    return x_mid, state


def kernel(x_prompt, x_sample, mem_prompt, cache_nsa_cmp_k, cache_nsa_cmp_v, cache_nsa_slc_k, cache_nsa_slc_v, cache_nsa_win_k, cache_nsa_win_v, cache_moba_k, cache_moba_v, cache_mem_k, cache_mem_v, page_table, norm_mix, norm_mem, w_in, w_mem_kv, cmp_w1_k, cmp_b1_k, cmp_w2_k, cmp_w1_v, cmp_b1_v, cmp_w2_v, w_nsa_out, w_moba_out, w_mem_out, w_merge_gate, b_merge_gate, w_out, norm_ffn, w_group, b_group, w_router, b_router, w_expert_gate, w_expert_up, w_expert_down, norm_final):
    depth = norm_mix.shape[0]
    xp, xs = x_prompt, x_sample
    Bp, S, D = xp.shape
    Bs, T, _ = xs.shape
    p_layers, s_layers = [], []
    for l in range(depth):
        lw = dict(norm_mix=norm_mix[l], norm_mem=norm_mem[l], w_in=w_in[l], w_mem_kv=w_mem_kv[l],
                  cmp_w1_k=cmp_w1_k[l], cmp_b1_k=cmp_b1_k[l], cmp_w2_k=cmp_w2_k[l],
                  cmp_w1_v=cmp_w1_v[l], cmp_b1_v=cmp_b1_v[l], cmp_w2_v=cmp_w2_v[l],
                  w_nsa_out=w_nsa_out[l], w_moba_out=w_moba_out[l], w_mem_out=w_mem_out[l],
                  w_merge_gate=w_merge_gate[l], b_merge_gate=b_merge_gate[l], w_out=w_out[l],
                  norm_ffn=norm_ffn[l], w_group=w_group[l], b_group=b_group[l],
                  w_router=w_router[l], b_router=b_router[l], w_expert_gate=w_expert_gate[l],
                  w_expert_up=w_expert_up[l], w_expert_down=w_expert_down[l])
        xp_mid, p_new = _prompt_layer(xp, mem_prompt, lw)
        xs_mid, s_new = _sample_layer(xs, cache_nsa_cmp_k[l], cache_nsa_cmp_v[l], cache_nsa_slc_k[l],
                                      cache_nsa_slc_v[l], cache_nsa_win_k[l], cache_nsa_win_v[l],
                                      cache_moba_k[l], cache_moba_v[l], cache_mem_k[l], cache_mem_v[l],
                                      page_table, lw)
        x_all = jnp.concatenate([xp_mid, xs_mid], axis=0)
        x_all = x_all + _moe(x_all, lw)
        xp = x_all[:Bp * S].reshape(Bp, S, D)
        xs = x_all[Bp * S:].reshape(Bs, T, D)
        p_layers.append(p_new)
        s_layers.append(s_new)
    y_prompt = _rms_norm(xp, norm_final)
    y_sample = _rms_norm(xs, norm_final)
    p_out = [jnp.stack(z) for z in zip(*p_layers)]
    s_out = [jnp.stack(z) for z in zip(*s_layers)]
    return (y_prompt, y_sample, *p_out, *s_out)
```

```python
import functools

import numpy as np
import jax
import jax.numpy as jnp
from jax import lax
from jax.experimental import pallas as pl
from jax.experimental.pallas import tpu as pltpu

f32 = jnp.float32
bf16 = jnp.bfloat16

D_MODEL = 1024
PAGE_SIZE = 128
HEAD_DIM = 64
NSA_HEADS = 8
NSA_GROUPS = 2
NSA_REP = NSA_HEADS // NSA_GROUPS
NSA_CMP_LEN = 32
NSA_CMP_STRIDE = 16
NSA_CMP_HIDDEN = 256
NSA_SEL_BLOCK = 64
NSA_TOPN = 16
NSA_WINDOW = 512
MOBA_HEADS = 8
MOBA_BLOCK = 256
MOBA_TOPK = 3
MEM_HEADS = 4
MEM_HEAD_DIM = 128
N_BRANCHES = 3
N_GROUPS = 4
EXPERTS_PER_GROUP = 8
N_EXPERTS = N_GROUPS * EXPERTS_PER_GROUP
TOP_K_IN_GROUP = 2
EXPERT_FF = 512
ROPE_THETA = 10000.0
NORM_EPS = 1e-6

NSA_Q_W = NSA_HEADS * HEAD_DIM
NSA_KV_W = NSA_GROUPS * HEAD_DIM
NSA_GATE_W = NSA_HEADS * 3
MOBA_W = MOBA_HEADS * HEAD_DIM
MEM_W = MEM_HEADS * MEM_HEAD_DIM
PROJ_WIDTH = NSA_Q_W + NSA_GATE_W + 6 * NSA_KV_W + 3 * MOBA_W + MEM_W

LANES = 128
VMEM_LIMIT = 56 << 20
MASKED = -1e30
ROW_MAX_INIT = -1e20
FORCED = 1e30
MOE_ROWS = 256

assert NSA_KV_W == LANES and 2 * HEAD_DIM == LANES and MEM_HEAD_DIM == LANES
NSA_HEAD_ORDER = tuple(g * NSA_REP + r for r in range(NSA_REP) for g in range(NSA_GROUPS))
STD_HEAD_ORDER = tuple(range(NSA_HEADS))


def _round_up(n, m):
    return -(-n // m) * m


def _params(*sem):
    return pltpu.CompilerParams(dimension_semantics=sem, vmem_limit_bytes=VMEM_LIMIT)


def _rms(xf, gain):
    return xf * lax.rsqrt(jnp.mean(xf * xf, axis=-1, keepdims=True) + NORM_EPS) * gain


def _dot(a, b):
    return jnp.dot(a, b, preferred_element_type=f32)


def _dot_nt(a, b):
    return lax.dot_general(a, b, (((1,), (1,)), ((), ())), preferred_element_type=f32)


def _split_hi_lo(a):
    hi = a.astype(bf16)
    lo = (a - hi.astype(f32)).astype(bf16)
    return hi, lo


def _dot_f32(a, b, dot=_dot):
    a_hi, a_lo = _split_hi_lo(a)
    b_hi, b_lo = _split_hi_lo(b)
    return dot(a_hi, b_hi) + dot(a_lo, b_hi) + dot(a_hi, b_lo)


def _mm_kernel(x_ref, w_ref, o_ref):
    o_ref[...] = _dot(x_ref[...].astype(bf16), w_ref[...]).astype(o_ref.dtype)


def _norm_mm_kernel(x_ref, g_ref, w_ref, o_ref, xn_ref):
    @pl.when(pl.program_id(1) == 0)
    def _():
        xn_ref[...] = _rms(x_ref[...], g_ref[...]).astype(bf16)

    o_ref[...] = _dot(xn_ref[...], w_ref[...]).astype(o_ref.dtype)


def _matmul(x, w, *, gain=None, out_dtype=f32, tm=512, tn=512, name="matmul"):
    M, K = x.shape
    N = w.shape[1]
    tm = min(tm, M)
    tn = max(t for t in range(LANES, min(tn, N) + 1, LANES) if N % t == 0)
    assert M % tm == 0 and N % LANES == 0, (M, N, tm, tn)
    grid = (M // tm, N // tn)
    x_spec = pl.BlockSpec((tm, K), lambda i, j: (i, 0))
    w_spec = pl.BlockSpec((K, tn), lambda i, j: (0, j))
    o_spec = pl.BlockSpec((tm, tn), lambda i, j: (i, j))
    out_shape = jax.ShapeDtypeStruct((M, N), out_dtype)
    if gain is None:
        return pl.pallas_call(_mm_kernel, grid=grid, in_specs=[x_spec, w_spec], out_specs=o_spec,
                              out_shape=out_shape, compiler_params=_params("parallel", "arbitrary"),
                              name=name)(x, w)
    return pl.pallas_call(
        _norm_mm_kernel, grid=grid,
        in_specs=[x_spec, pl.BlockSpec((1, K), lambda i, j: (0, 0)), w_spec],
        out_specs=o_spec, out_shape=out_shape, scratch_shapes=[pltpu.VMEM((tm, K), bf16)],
        compiler_params=_params("parallel", "arbitrary"), name=name)(x, gain.reshape(1, K), w)


_PROJ_SEGMENTS = (("nsa_q", 4, True), ("k_cmp", 1, True), ("v_cmp", 1, False), ("k_slc", 1, True),
                  ("v_slc", 1, False), ("k_win", 1, True), ("v_win", 1, False), ("moba_q", 4, True),
                  ("moba_k", 4, True), ("moba_v", 4, False), ("mem_q", 4, False), ("nsa_g", 1, False))
_PROJ_BLOCKS = sum(n for _, n, _ in _PROJ_SEGMENTS)


def _proj_kernel(x_ref, g_ref, w_ref, cos_ref, sin_ref, *o_refs, q_scale):
    xn = _rms(x_ref[...], g_ref[...]).astype(bf16)
    cos = cos_ref[...]
    sin = sin_ref[...]
    lane = lax.broadcasted_iota(jnp.int32, (1, LANES), 1)
    first_half = (lane & (HEAD_DIM // 2)) == 0

    def rope(a):
        partner = jnp.where(first_half, pltpu.roll(a, LANES - HEAD_DIM // 2, 1), pltpu.roll(a, HEAD_DIM // 2, 1))
        return a * cos + partner * sin

    c = 0
    for (name, n, rotary), o_ref in zip(_PROJ_SEGMENTS, o_refs):
        acc = _dot(xn, w_ref[:, c * LANES:(c + n) * LANES])
        c += n
        for r in range(n):
            a = acc[:, r * LANES:(r + 1) * LANES]
            if rotary:
                a = rope(a)
            if name == "nsa_q":
                a = a * q_scale
            if name == "nsa_g":
                a = jax.nn.sigmoid(a)
            o_ref[:, r * LANES:(r + 1) * LANES] = a.astype(o_ref.dtype)


def _proj_weight(w_in, head_order):
    D = w_in.shape[0]
    q = w_in[:, :NSA_Q_W].reshape(D, NSA_HEADS, HEAD_DIM)[:, np.array(head_order)].reshape(D, NSA_Q_W)
    gates = jnp.pad(w_in[:, NSA_Q_W:NSA_Q_W + NSA_GATE_W], ((0, 0), (0, LANES - NSA_GATE_W)))
    return jnp.concatenate([q, w_in[:, NSA_Q_W + NSA_GATE_W:], gates], axis=1).astype(bf16)


def _rope_tables(pos):
    half = HEAD_DIM // 2
    inv_freq = ROPE_THETA ** (-jnp.arange(half, dtype=f32) / half)
    ang = pos.astype(f32)[:, None] * inv_freq[None, :]
    cos, sin = jnp.cos(ang), jnp.sin(ang)
    reps = LANES // HEAD_DIM
    return jnp.tile(jnp.concatenate([cos, cos], axis=1), (1, reps)), jnp.tile(jnp.concatenate([-sin, sin], axis=1), (1, reps))


def _project(x, pos, lw, *, head_order, q_dtype, q_scale, tm=512):
    M, D = x.shape
    P = pos.shape[0]
    tm = min(tm, M, P)
    assert M % tm == 0 and P % tm == 0
    cos, sin = _rope_tables(pos)
    dtypes = dict(nsa_q=q_dtype, mem_q=bf16)
    out_shape = [jax.ShapeDtypeStruct((M, n * LANES), dtypes.get(name, f32)) for name, n, _ in _PROJ_SEGMENTS]
    out_specs = [pl.BlockSpec((tm, n * LANES), lambda i: (i, 0)) for _, n, _ in _PROJ_SEGMENTS]
    n_pos = P // tm
    outs = pl.pallas_call(
        functools.partial(_proj_kernel, q_scale=q_scale), grid=(M // tm,),
        in_specs=[pl.BlockSpec((tm, D), lambda i: (i, 0)), pl.BlockSpec((1, D), lambda i: (0, 0)),
                  pl.BlockSpec((D, _PROJ_BLOCKS * LANES), lambda i: (0, 0)),
                  pl.BlockSpec((tm, LANES), lambda i: (i % n_pos, 0)),
                  pl.BlockSpec((tm, LANES), lambda i: (i % n_pos, 0))],
        out_specs=out_specs, out_shape=out_shape, compiler_params=_params("parallel"),
        name="proj_in")(x, lw["norm_mix"].reshape(1, D), _proj_weight(lw["w_in"], head_order), cos, sin)
    return {name: o for (name, _, _), o in zip(_PROJ_SEGMENTS, outs)}


def _compress_kernel(rows_ref, nxt_ref, w1_ref, b1_ref, w2_ref, o_ref, buf_ref, *, n_chunks):
    stride = NSA_CMP_STRIDE
    n_rows = n_chunks * stride
    buf_ref[0:n_rows, :] = rows_ref[0]
    buf_ref[n_rows:n_rows + stride, :] = nxt_ref[0]
    acc = jnp.zeros((n_chunks, w1_ref.shape[-1]), f32)
    for p in range(0, NSA_CMP_LEN, 2):
        lhs = jnp.concatenate([buf_ref[pl.ds(p, n_chunks, stride=stride), :],
                               buf_ref[pl.ds(p + 1, n_chunks, stride=stride), :]], axis=1)
        acc = acc + _dot(lhs.astype(bf16), w1_ref[p // 2])
    hid = jax.nn.gelu(acc + b1_ref[...])
    o_ref[0] = _dot(hid.astype(bf16), w2_ref[...])


def _compress(rows, w1, b1, w2, *, chunks_per_step):
    B, L, W = rows.shape
    G, dh, stride = NSA_GROUPS, HEAD_DIM, NSA_CMP_STRIDE
    n_total = L // stride
    C = min(chunks_per_step, n_total)
    assert W == G * dh and L % stride == 0 and n_total % C == 0 and NSA_CMP_LEN == 2 * stride
    hidden = w1.shape[1]
    eye = jnp.eye(G, dtype=f32)
    w1p = w1.reshape(NSA_CMP_LEN, dh, hidden)
    w1bd = jnp.einsum("gh,pdn->pgdhn", eye, w1p).reshape(NSA_CMP_LEN // 2, 2 * W, G * hidden).astype(bf16)
    w2bd = jnp.einsum("gh,nd->gnhd", eye, w2).reshape(G * hidden, W).astype(bf16)
    b1bd = jnp.tile(b1, G).reshape(1, G * hidden)
    n_steps = n_total // C
    last_blk = L // stride - 1
    return pl.pallas_call(
        functools.partial(_compress_kernel, n_chunks=C), grid=(B, n_steps),
        in_specs=[pl.BlockSpec((1, C * stride, W), lambda b, i: (b, i, 0)),
                  pl.BlockSpec((1, stride, W), lambda b, i: (b, jnp.minimum((i + 1) * C, last_blk), 0)),
                  pl.BlockSpec(w1bd.shape, lambda b, i: (0, 0, 0)),
                  pl.BlockSpec(b1bd.shape, lambda b, i: (0, 0)),
                  pl.BlockSpec(w2bd.shape, lambda b, i: (0, 0))],
        out_specs=pl.BlockSpec((1, C, W), lambda b, i: (b, i, 0)),
        out_shape=jax.ShapeDtypeStruct((B, n_total, W), f32),
        scratch_shapes=[pltpu.VMEM((C * stride + stride, W), f32)],
        compiler_params=_params("parallel", "arbitrary"), name="compress")(rows, rows, w1bd, b1bd, w2bd)


def _flash_tile(q, kt, vt, mask, m, l, acc, rep):
    tq, tk = mask.shape
    s = _dot_nt(q, kt.astype(bf16)).reshape(rep, tq, tk)
    s = jnp.where(mask[None], s, MASKED)
    m_new = jnp.maximum(m, jnp.max(s, axis=-1, keepdims=True))
    p = jnp.exp(s - m_new)
    alpha = jnp.exp(m - m_new)
    l = alpha * l + jnp.sum(p, axis=-1, keepdims=True)
    pv = _dot(p.reshape(rep * tq, tk).astype(bf16), vt.astype(bf16))
    acc = alpha.reshape(rep * tq, 1) * acc + pv
    return m_new, l, acc


def _flash_init(rep, tq):
    return (jnp.full((rep, tq, 1), ROW_MAX_INIT, f32), jnp.zeros((rep, tq, 1), f32),
            jnp.zeros((rep * tq, LANES), f32))


def _flash_finish(l, acc):
    l = l.reshape(acc.shape[0], 1)
    return acc / jnp.where(l > 0, l, 1.0)


def _topk_mask(score, k, col_index, n_real):
    rank = jnp.zeros(score.shape, jnp.int32)
    for i in range(n_real):
        si = score[:, i:i + 1]
        ahead = (si > score) | ((si == score) & (col_index > i))
        rank = rank + jnp.where(ahead, 1, 0)
    return rank < k


def _mask_to_bf16(mask):
    return jnp.where(mask, 1.0, 0.0).astype(bf16)


def _nsa_prompt_kernel(q_ref, g_ref, ck_ref, cv_ref, k_ref, v_ref, wk_ref, wv_ref,
                       imp_ref, esel_ref, o_ref, selexp_ref, *, tq, tk, tkw, n_sel):
    rep = NSA_REP
    s0 = pl.program_id(1) * tq
    qpos = s0 + lax.broadcasted_iota(jnp.int32, (tq, 1), 0)
    n_cmp = ck_ref.shape[1]
    cmp_end = lax.broadcasted_iota(jnp.int32, (1, n_cmp), 1) * NSA_CMP_STRIDE + (NSA_CMP_LEN - 1)
    cmp_mask = cmp_end <= qpos
    lane = lax.broadcasted_iota(jnp.int32, (1, LANES), 1)
    cur = jnp.right_shift(qpos, NSA_SEL_BLOCK.bit_length() - 1)
    visible = lane <= cur
    forced = (lane == 0) | (lane == cur) | (lane == cur - 1)
    ck = ck_ref[0].astype(bf16)
    cv = cv_ref[0].astype(bf16)

    outs = []
    for g in range(NSA_GROUPS):
        in_group = (lane >= g * HEAD_DIM) & (lane < (g + 1) * HEAD_DIM)
        q = jnp.concatenate(
            [jnp.where(in_group, q_ref[0, :, r * LANES:(r + 1) * LANES], jnp.zeros((), bf16)) for r in range(rep)],
            axis=0)
        s = jnp.where(cmp_mask[None], _dot_nt(q, ck).reshape(rep, tq, n_cmp), MASKED)
        m = jnp.maximum(jnp.max(s, axis=-1, keepdims=True), ROW_MAX_INIT)
        e = jnp.exp(s - m)
        den = jnp.sum(e, axis=-1, keepdims=True)
        p_c = e / jnp.where(den > 0, den, 1.0)
        o_c = _dot(p_c.reshape(rep * tq, n_cmp).astype(bf16), cv)
        p_hi, p_lo = _split_hi_lo(jnp.sum(p_c, axis=0))
        imp = _dot(p_hi, imp_ref[...]) + _dot(p_lo, imp_ref[...])
        score = jnp.where(visible, jnp.where(forced, FORCED, imp), -FORCED)
        sel = _topk_mask(score, NSA_TOPN, lane, n_sel) & visible
        selexp_ref[...] = _dot(_mask_to_bf16(sel), esel_ref[...])

        def slc_body(t, carry):
            k0 = pl.multiple_of(t * tk, tk)
            kpos = k0 + lax.broadcasted_iota(jnp.int32, (1, tk), 1)
            mask = (selexp_ref[:, pl.ds(k0, tk)] > 0.5) & (kpos <= qpos)
            return _flash_tile(q, k_ref[0, pl.ds(k0, tk), :], v_ref[0, pl.ds(k0, tk), :], mask, *carry, rep)

        _, l, acc = lax.fori_loop(0, (s0 + tq + tk - 1) // tk, slc_body, _flash_init(rep, tq))
        o_s = _flash_finish(l, acc)

        def win_body(t, carry):
            k0 = pl.multiple_of(t * tkw, tkw)
            kpos = k0 + lax.broadcasted_iota(jnp.int32, (1, tkw), 1)
            mask = (kpos <= qpos) & (kpos > qpos - NSA_WINDOW)
            return _flash_tile(q, wk_ref[0, pl.ds(k0, tkw), :], wv_ref[0, pl.ds(k0, tkw), :], mask, *carry, rep)

        w_lo = jnp.maximum(s0 - NSA_WINDOW, 0) // tkw
        _, l, acc = lax.fori_loop(w_lo, (s0 + tq) // tkw, win_body, _flash_init(rep, tq))
        o_w = _flash_finish(l, acc)

        per_rep = []
        for r in range(rep):
            c = 3 * (g * rep + r)
            rows = slice(r * tq, (r + 1) * tq)
            per_rep.append(g_ref[0, :, c:c + 1] * o_c[rows] + g_ref[0, :, c + 1:c + 2] * o_s[rows]
                           + g_ref[0, :, c + 2:c + 3] * o_w[rows])
        outs.append(per_rep)

    for r in range(rep):
        o_ref[0, :, r * LANES:(r + 1) * LANES] = jnp.where(lane < HEAD_DIM, outs[0][r], outs[1][r]).astype(o_ref.dtype)


def _nsa_static_maps(n_cmp_pad, n_sel):
    ratio_c = NSA_CMP_LEN // NSA_CMP_STRIDE
    ratio_s = NSA_SEL_BLOCK // NSA_CMP_STRIDE
    nc = n_cmp_pad - ratio_c + 1
    mat = np.zeros((n_cmp_pad, n_sel), np.float32)
    j = np.arange(n_sel)
    for mm in range(ratio_s):
        for n in range(ratio_c):
            i = ratio_s * j + mm - n
            ok = (i >= 0) & (i < nc)
            np.add.at(mat, (i[ok], j[ok]), 1.0)
    return mat


def _block_expand(n_blocks, block, total):
    k = np.arange(total)
    return (k[None, :] // block == np.arange(n_blocks)[:, None]).astype(np.float32)


def _nsa_prompt(q, gates, c_k, c_v, k_slc, v_slc, k_win, v_win, *, tq=128, tk=256, tkw=128):
    B, S, _ = q.shape
    n_cmp = S // NSA_CMP_STRIDE
    n_sel = max(S // NSA_SEL_BLOCK, NSA_TOPN)
    assert S % tq == 0 and S % tk == 0 and tq % tkw == 0 and NSA_WINDOW % tkw == 0
    assert n_sel * NSA_SEL_BLOCK == S and n_sel <= LANES and c_k.shape[1] == n_cmp
    imp = jnp.asarray(np.pad(_nsa_static_maps(n_cmp, n_sel), ((0, 0), (0, LANES - n_sel))), bf16)
    esel = jnp.asarray(_block_expand(LANES, NSA_SEL_BLOCK, S), bf16)
    per_b = lambda rows: pl.BlockSpec((1, rows, LANES), lambda b, i: (b, 0, 0))
    full = lambda shape: pl.BlockSpec(shape, lambda b, i: (0,) * len(shape))
    return pl.pallas_call(
        functools.partial(_nsa_prompt_kernel, tq=tq, tk=tk, tkw=tkw, n_sel=n_sel),
        grid=(B, S // tq),
        in_specs=[pl.BlockSpec((1, tq, NSA_Q_W), lambda b, i: (b, i, 0)),
                  pl.BlockSpec((1, tq, LANES), lambda b, i: (b, i, 0)),
                  per_b(n_cmp), per_b(n_cmp), per_b(S), per_b(S), per_b(S), per_b(S),
                  full((n_cmp, LANES)), full((LANES, S))],
        out_specs=pl.BlockSpec((1, tq, NSA_Q_W), lambda b, i: (b, i, 0)),
        out_shape=jax.ShapeDtypeStruct((B, S, NSA_Q_W), bf16),
        scratch_shapes=[pltpu.VMEM((tq, S), f32)],
        compiler_params=_params("parallel", "arbitrary"),
        name="nsa_prompt")(q, gates, c_k, c_v, k_slc, v_slc, k_win, v_win, imp, esel)


def _moba_prompt_kernel(q_ref, k_ref, v_ref, ablk_ref, esel_ref, o_ref, means_ref, selexp_ref, *, tq, n_blk):
    tk = MOBA_BLOCK
    n_pairs = q_ref.shape[-1] // LANES
    qi = pl.program_id(1)
    s0 = qi * tq
    cur = s0 // MOBA_BLOCK
    qpos = s0 + (lax.broadcasted_iota(jnp.int32, (2 * tq, 1), 0) & (tq - 1))
    lane = lax.broadcasted_iota(jnp.int32, (1, LANES), 1)
    low = lane < HEAD_DIM
    past = lane < cur

    @pl.when(qi == 0)
    def _():
        for j in range(n_pairs):
            k_hi, k_lo = _split_hi_lo(k_ref[0, :, j * LANES:(j + 1) * LANES])
            means_ref[j] = _dot(ablk_ref[...], k_hi) + _dot(ablk_ref[...], k_lo)

    for j in range(n_pairs):
        cols = slice(j * LANES, (j + 1) * LANES)
        qj = q_ref[0, :, cols]
        qf = jnp.concatenate([jnp.where(low, qj, 0.0), jnp.where(low, 0.0, qj)], axis=0)
        gate = _dot_f32(qf, means_ref[j], _dot_nt)
        score = jnp.where(past, gate, -FORCED)
        sel = (_topk_mask(score, MOBA_TOPK, lane, n_blk) & past) | (lane == cur)
        selexp_ref[...] = _dot(_mask_to_bf16(sel), esel_ref[...])
        q = (qf * (HEAD_DIM ** -0.5)).astype(bf16)

        def body(t, carry):
            k0 = pl.multiple_of(t * tk, tk)
            kpos = k0 + lax.broadcasted_iota(jnp.int32, (1, tk), 1)
            mask = (selexp_ref[:, pl.ds(k0, tk)] > 0.5) & (kpos <= qpos)
            return _flash_tile(q, k_ref[0, pl.ds(k0, tk), cols], v_ref[0, pl.ds(k0, tk), cols], mask, *carry, 1)

        _, l, acc = lax.fori_loop(0, cur + 1, body, _flash_init(1, 2 * tq))
        o = _flash_finish(l, acc)
        o_ref[0, :, cols] = jnp.where(low, o[:tq], o[tq:]).astype(o_ref.dtype)


def _moba_prompt(q, k, v, *, tq=256):
    B, S, W = q.shape
    n_blk = S // MOBA_BLOCK
    assert S % MOBA_BLOCK == 0 and MOBA_BLOCK % tq == 0 and tq & (tq - 1) == 0
    assert MOBA_TOPK <= n_blk <= LANES and W % LANES == 0
    esel_np = _block_expand(LANES, MOBA_BLOCK, S)
    ablk = jnp.asarray(esel_np / MOBA_BLOCK, bf16)
    esel = jnp.asarray(esel_np, bf16)
    per_b = pl.BlockSpec((1, S, W), lambda b, i: (b, 0, 0))
    full = pl.BlockSpec((LANES, S), lambda b, i: (0, 0))
    return pl.pallas_call(
        functools.partial(_moba_prompt_kernel, tq=tq, n_blk=n_blk),
        grid=(B, S // tq),
        in_specs=[pl.BlockSpec((1, tq, W), lambda b, i: (b, i, 0)), per_b, per_b, full, full],
        out_specs=pl.BlockSpec((1, tq, W), lambda b, i: (b, i, 0)),
        out_shape=jax.ShapeDtypeStruct((B, S, W), bf16),
        scratch_shapes=[pltpu.VMEM((W // LANES, LANES, LANES), f32), pltpu.VMEM((2 * tq, S), f32)],
        compiler_params=_params("parallel", "arbitrary"),
        name="moba_prompt")(q, k, v, ablk, esel)


def _mem_attn_kernel(q_ref, kv_ref, o_ref):
    d = MEM_HEAD_DIM
    for h in range(MEM_HEADS):
        k = kv_ref[0, :, h * d:(h + 1) * d].astype(bf16)
        v = kv_ref[0, :, MEM_W + h * d:MEM_W + (h + 1) * d].astype(bf16)
        s = _dot_nt(q_ref[0, :, h * d:(h + 1) * d], k) * (d ** -0.5)
        e = jnp.exp(s - jnp.max(s, axis=-1, keepdims=True))
        p = e / jnp.sum(e, axis=-1, keepdims=True)
        o_ref[0, :, h * d:(h + 1) * d] = _dot(p.astype(bf16), v).astype(o_ref.dtype)


def _mem_attn(q, kv, *, tq=512):
    B, T, W = q.shape
    M = kv.shape[1]
    tq = min(tq, T)
    assert T % tq == 0
    return pl.pallas_call(
        _mem_attn_kernel, grid=(B, T // tq),
        in_specs=[pl.BlockSpec((1, tq, W), lambda b, i: (b, i, 0)),
                  pl.BlockSpec((1, M, 2 * W), lambda b, i: (b, 0, 0))],
        out_specs=pl.BlockSpec((1, tq, W), lambda b, i: (b, i, 0)),
        out_shape=jax.ShapeDtypeStruct((B, T, W), bf16),
        compiler_params=_params("parallel", "arbitrary"), name="mem_attn")(q, kv)


def _merge_kernel(x_ref, gmix_ref, on_ref, om_ref, oe_ref, wg_ref, bg_ref, wn_ref, wm_ref, we_ref, wo_ref,
                  gffn_ref, wr_ref, br_ref, xmid_ref, h_ref, logit_ref):
    D = x_ref.shape[1]
    xf = x_ref[...]
    hn = _rms(xf, gmix_ref[...]).astype(bf16)
    mixed = None
    for i, (o_ref, w_ref) in enumerate(((on_ref, wn_ref), (om_ref, wm_ref), (oe_ref, we_ref))):
        gate = jax.nn.sigmoid(_dot(hn, wg_ref[:, i * D:(i + 1) * D]) + bg_ref[:, i * D:(i + 1) * D])
        term = gate * _dot(o_ref[...], w_ref[...])
        mixed = term if mixed is None else mixed + term
    xm = xf + _dot(mixed.astype(bf16), wo_ref[...])
    xmid_ref[...] = xm
    y = _rms(xm, gffn_ref[...])
    h_ref[...] = y.astype(bf16)
    logit_ref[...] = _dot_f32(y, wr_ref[...]) + br_ref[...]


def _merge(x, o_nsa, o_moba, o_mem, lw, *, nsa_head_order, tm=512):
    M, D = x.shape
    tm = min(tm, M)
    assert M % tm == 0
    n_route = N_GROUPS + N_EXPERTS
    w_route = jnp.pad(jnp.concatenate([lw["w_group"], lw["w_router"]], axis=1), ((0, 0), (0, LANES - n_route)))
    b_route = jnp.pad(jnp.concatenate([lw["b_group"], lw["b_router"]]), (0, LANES - n_route)).reshape(1, LANES)
    w_nsa = lw["w_nsa_out"].reshape(NSA_HEADS, HEAD_DIM, D)[np.array(nsa_head_order)].reshape(NSA_Q_W, D)
    row = lambda w: pl.BlockSpec((tm, w), lambda i: (i, 0))
    const = lambda a: pl.BlockSpec(a.shape, lambda i: (0,) * a.ndim)
    consts = [lw["w_merge_gate"].astype(bf16), lw["b_merge_gate"].reshape(1, -1), w_nsa.astype(bf16),
              lw["w_moba_out"].astype(bf16), lw["w_mem_out"].astype(bf16), lw["w_out"].astype(bf16),
              lw["norm_ffn"].reshape(1, D), w_route, b_route]
    gmix = lw["norm_mix"].reshape(1, D)
    return pl.pallas_call(
        _merge_kernel, grid=(M // tm,),
        in_specs=[row(D), const(gmix), row(NSA_Q_W), row(MOBA_W), row(MEM_W)] + [const(a) for a in consts],
        out_specs=[row(D), row(D), row(LANES)],
        out_shape=[jax.ShapeDtypeStruct((M, D), f32), jax.ShapeDtypeStruct((M, D), bf16),
                   jax.ShapeDtypeStruct((M, LANES), f32)],
        compiler_params=_params("parallel"), name="merge")(x, gmix, o_nsa, o_moba, o_mem, *consts)


def _expert_kernel(be_ref, x_ref, wg_ref, wu_ref, wd_ref, o_ref):
    x = x_ref[...]
    a = _dot(x, wg_ref[0])
    u = _dot(x, wu_ref[0])
    mid = (a * jax.nn.sigmoid(a)) * u
    o_ref[...] = _dot(mid.astype(bf16), wd_ref[0])


def _expert_ffn(blk_expert, xs, w_gate, w_up, w_down):
    cap, D = xs.shape
    F = w_gate.shape[-1]
    return pl.pallas_call(
        _expert_kernel,
        grid_spec=pltpu.PrefetchScalarGridSpec(
            num_scalar_prefetch=1, grid=(cap // MOE_ROWS,),
            in_specs=[pl.BlockSpec((MOE_ROWS, D), lambda i, be: (i, 0)),
                      pl.BlockSpec((1, D, F), lambda i, be: (be[i], 0, 0)),
                      pl.BlockSpec((1, D, F), lambda i, be: (be[i], 0, 0)),
                      pl.BlockSpec((1, F, D), lambda i, be: (be[i], 0, 0))],
            out_specs=pl.BlockSpec((MOE_ROWS, D), lambda i, be: (i, 0))),
        out_shape=jax.ShapeDtypeStruct((cap, D), f32),
        compiler_params=_params("arbitrary"), name="expert_ffn")(blk_expert, xs, w_gate, w_up, w_down)


def _moe_rows(h_parts, logits, lw):
    n_tok, D = logits.shape[0], h_parts[0].shape[1]
    n_route = N_GROUPS + N_EXPERTS
    g_logits = logits[:, :N_GROUPS]
    g_prob = jax.nn.softmax(g_logits, axis=-1)
    grp = jnp.argmax(g_logits, axis=-1).astype(jnp.int32)
    e_logits = logits[:, N_GROUPS:n_route].reshape(-1, N_GROUPS, EXPERTS_PER_GROUP)
    e_in = jnp.take_along_axis(e_logits, grp[:, None, None], axis=1)[:, 0]
    top_p, top_i = lax.top_k(jax.nn.softmax(e_in, axis=-1), TOP_K_IN_GROUP)
    weight = top_p / jnp.sum(top_p, axis=-1, keepdims=True) * jnp.take_along_axis(g_prob, grp[:, None], axis=1)
    expert = grp[:, None] * EXPERTS_PER_GROUP + top_i.astype(jnp.int32)
    n_assign = n_tok * TOP_K_IN_GROUP
    n_blocks = -(-n_assign // MOE_ROWS) + N_EXPERTS
    cap = n_blocks * MOE_ROWS
    flat_e = expert.reshape(-1)
    flat_t = jnp.repeat(jnp.arange(n_tok, dtype=jnp.int32), TOP_K_IN_GROUP)
    onehot = (flat_e[:, None] == jnp.arange(N_EXPERTS, dtype=jnp.int32)[None, :]).astype(jnp.int32)
    csum = jnp.cumsum(onehot, axis=0)
    rank = jnp.take_along_axis(csum, flat_e[:, None], axis=1)[:, 0] - 1
    counts = csum[-1]
    padded = (counts + MOE_ROWS - 1) // MOE_ROWS * MOE_ROWS
    pad_end = jnp.cumsum(padded)
    dest = (pad_end - padded)[flat_e] + rank
    h_pad = jnp.concatenate(list(h_parts) + [jnp.zeros((1, D), h_parts[0].dtype)], axis=0)
    slot_tok = jnp.full((cap,), n_tok, jnp.int32).at[dest].set(flat_t)
    blk_expert = jnp.minimum(
        jnp.searchsorted(pad_end, jnp.arange(n_blocks, dtype=jnp.int32) * MOE_ROWS, side="right"),
        N_EXPERTS - 1).astype(jnp.int32)
    ys = _expert_ffn(blk_expert, h_pad[slot_tok], lw["w_expert_gate"].astype(bf16),
                     lw["w_expert_up"].astype(bf16), lw["w_expert_down"].astype(bf16))
    return ys[dest].reshape(n_tok, TOP_K_IN_GROUP * D), weight


def _combine_kernel(x_ref, p_ref, w_ref, g_ref, o_ref, *, normalise):
    D = x_ref.shape[1]
    y = x_ref[...] + (w_ref[:, 0:1] * p_ref[:, :D] + w_ref[:, 1:2] * p_ref[:, D:])
    o_ref[...] = _rms(y, g_ref[...]) if normalise else y


def _combine(x, picked, weight, gain, *, normalise, tm=512):
    M, D = x.shape
    tm = min(tm, M)
    assert M % tm == 0 and TOP_K_IN_GROUP == 2
    return pl.pallas_call(
        functools.partial(_combine_kernel, normalise=normalise), grid=(M // tm,),
        in_specs=[pl.BlockSpec((tm, D), lambda i: (i, 0)), pl.BlockSpec((tm, 2 * D), lambda i: (i, 0)),
                  pl.BlockSpec((tm, 2), lambda i: (i, 0)), pl.BlockSpec((1, D), lambda i: (0, 0))],
        out_specs=pl.BlockSpec((tm, D), lambda i: (i, 0)),
        out_shape=jax.ShapeDtypeStruct((M, D), f32),
        compiler_params=_params("parallel"), name="combine")(x, picked, weight, gain.reshape(1, D))


def _paged_gather_kernel(pt_ref, pool_ref, out_ref, sem):
    b = pl.program_id(0)
    n_pages = out_ref.shape[1]

    def page_copy(p):
        return pltpu.make_async_copy(pool_ref.at[pt_ref[b, p]], out_ref.at[b, p], sem)

    def start(p, c):
        page_copy(p).start()
        return c

    def wait(p, c):
        page_copy(p).wait()
        return c

    lax.fori_loop(0, n_pages, start, 0)
    lax.fori_loop(0, n_pages, wait, 0)


def _paged_gather(pool, page_table):
    B, n_pages = page_table.shape
    _, page, W = pool.shape
    out = pl.pallas_call(
        _paged_gather_kernel,
        grid_spec=pltpu.PrefetchScalarGridSpec(
            num_scalar_prefetch=1, grid=(B,),
            in_specs=[pl.BlockSpec(memory_space=pl.ANY)],
            out_specs=pl.BlockSpec(memory_space=pl.ANY),
            scratch_shapes=[pltpu.SemaphoreType.DMA(())]),
        out_shape=jax.ShapeDtypeStruct((B, n_pages, page, W), pool.dtype),
        compiler_params=_params("arbitrary"), name="paged_gather")(page_table, pool)
    return out.reshape(B, n_pages * page, W)


def _prompt_layer(x, mem, lw):
    B, S, D = x.shape
    pos = jnp.arange(S, dtype=jnp.int32)
    pr = _project(x.reshape(B * S, D), pos, lw, head_order=NSA_HEAD_ORDER, q_dtype=bf16, q_scale=HEAD_DIM ** -0.5)
    seq = lambda name: pr[name].reshape(B, S, -1)
    Bm, M, _ = mem.shape
    kv = _matmul(mem.reshape(Bm * M, D), lw["w_mem_kv"].astype(bf16), gain=lw["norm_mem"], name="mem_kv")
    kv = kv.reshape(Bm, M, 2 * MEM_W)
    c_k = _compress(seq("k_cmp"), lw["cmp_w1_k"], lw["cmp_b1_k"], lw["cmp_w2_k"], chunks_per_step=128)
    c_v = _compress(seq("v_cmp"), lw["cmp_w1_v"], lw["cmp_b1_v"], lw["cmp_w2_v"], chunks_per_step=128)
    o_nsa = _nsa_prompt(seq("nsa_q"), seq("nsa_g"), c_k, c_v, seq("k_slc"), seq("v_slc"), seq("k_win"), seq("v_win"))
    o_moba = _moba_prompt(seq("moba_q"), seq("moba_k"), seq("moba_v"))
    o_mem = _mem_attn(seq("mem_q"), kv)
    merged = _merge(x.reshape(B * S, D), o_nsa.reshape(B * S, -1), o_moba.reshape(B * S, -1),
                    o_mem.reshape(B * S, -1), lw, nsa_head_order=NSA_HEAD_ORDER)
    wb = min(NSA_WINDOW, S)
    heads = lambda name, n, d: pr[name].reshape(B, S, n, d)
    state = (heads("k_cmp", NSA_GROUPS, HEAD_DIM), heads("v_cmp", NSA_GROUPS, HEAD_DIM),
             heads("k_slc", NSA_GROUPS, HEAD_DIM), heads("v_slc", NSA_GROUPS, HEAD_DIM),
             heads("k_win", NSA_GROUPS, HEAD_DIM)[:, S - wb:], heads("v_win", NSA_GROUPS, HEAD_DIM)[:, S - wb:],
             heads("moba_k", MOBA_HEADS, HEAD_DIM), heads("moba_v", MOBA_HEADS, HEAD_DIM),
             kv[:, :, :MEM_W].reshape(Bm, M, MEM_HEADS, MEM_HEAD_DIM),
             kv[:, :, MEM_W:].reshape(Bm, M, MEM_HEADS, MEM_HEAD_DIM))
    return merged, state


def _masked_softmax(scores, mask):
    s = jnp.where(mask, scores, -jnp.inf)
    m = jnp.max(s, axis=-1, keepdims=True)
    e = jnp.exp(s - jnp.where(jnp.isfinite(m), m, 0.0))
    dsum = jnp.sum(e, axis=-1, keepdims=True)
    return e / jnp.where(dsum > 0, dsum, 1.0)


def _contiguous_fetch(rows):
    total = rows.shape[1]

    def fetch(pos, hidx):
        return jax.vmap(lambda r, p, g: r[p, g])(rows, jnp.clip(pos, 0, total - 1), hidx)
    return fetch


def _paged_fetch(pool, page_table, new_rows):
    n_seq, n_pages = page_table.shape
    past_len = n_pages * PAGE_SIZE
    n_new = new_rows.shape[1]
    total = past_len + n_new

    def fetch(pos, hidx):
        pos = jnp.clip(pos, 0, total - 1)
        in_past = pos < past_len
        pp = jnp.minimum(pos, past_len - 1)
        logical = (pp // PAGE_SIZE).reshape(n_seq, -1)
        phys = jnp.take_along_axis(page_table, logical, axis=1).reshape(pos.shape)
        from_pool = pool[phys, pp % PAGE_SIZE, hidx]
        npos = jnp.clip(pos - past_len, 0, n_new - 1)
        from_new = jax.vmap(lambda r, p, g: r[p, g])(new_rows, npos, hidx)
        return jnp.where(in_past[..., None], from_pool, from_new)
    return fetch


def _sample_nsa(q, gates, q_pos, cmp_k, cmp_v, total, fetch_k, fetch_v, win_k, win_v, win_pos):
    B, T = q.shape[:2]
    scale = HEAD_DIM ** -0.5
    nc = cmp_k.shape[1]
    nsb = max(-(-total // NSA_SEL_BLOCK), NSA_TOPN)
    imp_map = jnp.asarray(_nsa_static_maps(nc + 1, nsb)[:nc])
    cmp_end = jnp.arange(nc, dtype=jnp.int32) * NSA_CMP_STRIDE + NSA_CMP_LEN - 1
    qg = q.reshape(B, T, NSA_GROUPS, NSA_REP, HEAD_DIM)
    qp = q_pos[:, None]
    s = jnp.einsum("btgrd,bcgd->btgrc", qg, cmp_k, preferred_element_type=f32) * scale
    p_c = _masked_softmax(s, (cmp_end[None, :] <= qp)[None, :, None, None, :])
    o_c = jnp.einsum("btgrc,bcgd->btgrd", p_c, cmp_v)
    imp = jnp.einsum("btgc,cj->btgj", p_c.sum(axis=3), imp_map, precision=lax.Precision.HIGHEST)
    j = jnp.arange(nsb)[None, :]
    cur = (q_pos // NSA_SEL_BLOCK)[:, None]
    visible = (j <= cur)[None, :, None, :]
    forced = ((j == 0) | (j == cur) | (j == cur - 1))[None, :, None, :]
    score = jnp.where(visible, jnp.where(forced, jnp.inf, imp), -jnp.inf)
    top_s, top_j = lax.top_k(score, NSA_TOPN)
    kpos = (top_j[..., None] * NSA_SEL_BLOCK + jnp.arange(NSA_SEL_BLOCK)).reshape(B, T, NSA_GROUPS, -1)
    kmask = jnp.repeat(top_s > -jnp.inf, NSA_SEL_BLOCK, axis=-1) & (kpos <= q_pos[None, :, None, None])
    gidx = jnp.broadcast_to(jnp.arange(NSA_GROUPS)[None, None, :, None], kpos.shape)
    ks = fetch_k(kpos, gidx)
    vs = fetch_v(kpos, gidx)
    s = jnp.einsum("btgrd,btgkd->btgrk", qg, ks, preferred_element_type=f32) * scale
    p_s = _masked_softmax(s, kmask[:, :, :, None, :])
    o_s = jnp.einsum("btgrk,btgkd->btgrd", p_s, vs)
    kp = win_pos[None, :]
    wmask = ((kp <= qp) & (kp > qp - NSA_WINDOW) & (kp >= 0))[None, :, None, None, :]
    s = jnp.einsum("btgrd,bkgd->btgrk", qg, win_k, preferred_element_type=f32) * scale
    p_w = _masked_softmax(s, wmask)
    o_w = jnp.einsum("btgrk,bkgd->btgrd", p_w, win_v)
    g = gates.reshape(B, T, NSA_GROUPS, NSA_REP, 3)
    o = g[..., 0:1] * o_c + g[..., 1:2] * o_s + g[..., 2:3] * o_w
    return o.reshape(B, T, NSA_Q_W)


def _sample_moba(q, q_pos, k_all, fetch_k, fetch_v):
    B, T, H, dh = q.shape
    L = k_all.shape[1]
    nb = L // MOBA_BLOCK
    means = k_all[:, : nb * MOBA_BLOCK].reshape(B, nb, MOBA_BLOCK, H, dh).mean(axis=2)
    gate = jnp.einsum("bthd,bjhd->bthj", q, means, preferred_element_type=f32, precision=lax.Precision.HIGHEST)
    cur = q_pos // MOBA_BLOCK
    past = (jnp.arange(nb)[None, :] < cur[:, None])[None, :, None, :]
    top_s, top_j = lax.top_k(jnp.where(past, gate, -jnp.inf), MOBA_TOPK)
    blocks = jnp.concatenate([top_j, jnp.broadcast_to(cur[None, :, None, None], (B, T, H, 1))], axis=-1)
    bvalid = jnp.concatenate([top_s > -jnp.inf, jnp.ones((B, T, H, 1), bool)], axis=-1)
    kpos = (blocks[..., None] * MOBA_BLOCK + jnp.arange(MOBA_BLOCK)).reshape(B, T, H, -1)
    kmask = jnp.repeat(bvalid, MOBA_BLOCK, axis=-1) & (kpos <= q_pos[None, :, None, None])
    hidx = jnp.broadcast_to(jnp.arange(H)[None, None, :, None], kpos.shape)
    ks = fetch_k(kpos, hidx)
    vs = fetch_v(kpos, hidx)
    s = jnp.einsum("bthd,bthkd->bthk", q, ks, preferred_element_type=f32) * (dh ** -0.5)
    p = _masked_softmax(s, kmask)
    o = jnp.einsum("bthk,bthkd->bthd", p, vs)
    return o.reshape(B, T, H * dh)


def _sample_layer(x, c_cmp_k, c_cmp_v, c_slc_k, c_slc_v, c_win_k, c_win_v,
                  c_moba_k, c_moba_v, c_mem_k, c_mem_v, page_table, lw):
    B, T, D = x.shape
    past_len = page_table.shape[1] * PAGE_SIZE
    total = past_len + T
    pos = past_len + jnp.arange(T, dtype=jnp.int32)
    assert T == 1
    pr = _project(x.reshape(B, D), jnp.broadcast_to(pos, (B,)), lw, head_order=STD_HEAD_ORDER, q_dtype=f32, q_scale=1.0)
    heads = lambda name, n, d: pr[name].reshape(B, T, n, d)
    nsa_q = heads("nsa_q", NSA_HEADS, HEAD_DIM)
    nsa_g = pr["nsa_g"][:, :NSA_GATE_W].reshape(B, T, NSA_HEADS, 3)
    k_cmp, v_cmp = heads("k_cmp", NSA_GROUPS, HEAD_DIM), heads("v_cmp", NSA_GROUPS, HEAD_DIM)
    k_slc, v_slc = heads("k_slc", NSA_GROUPS, HEAD_DIM), heads("v_slc", NSA_GROUPS, HEAD_DIM)
    k_win, v_win = heads("k_win", NSA_GROUPS, HEAD_DIM), heads("v_win", NSA_GROUPS, HEAD_DIM)
    moba_q, moba_k, moba_v = (heads(n, MOBA_HEADS, HEAD_DIM) for n in ("moba_q", "moba_k", "moba_v"))
    flat = lambda pool: pool.reshape(pool.shape[0], pool.shape[1], -1)
    assert T < NSA_CMP_STRIDE and past_len % NSA_CMP_STRIDE == 0
    n_cmp = past_len // NSA_CMP_STRIDE - 1
    cmp4 = lambda a: a[:, :n_cmp].reshape(B, n_cmp, NSA_GROUPS, HEAD_DIM)
    c_k = cmp4(_compress(_paged_gather(flat(c_cmp_k), page_table), lw["cmp_w1_k"], lw["cmp_b1_k"], lw["cmp_w2_k"],
                         chunks_per_step=512))
    c_v = cmp4(_compress(_paged_gather(flat(c_cmp_v), page_table), lw["cmp_w1_v"], lw["cmp_b1_v"], lw["cmp_w2_v"],
                         chunks_per_step=512))
    moba_past = _paged_gather(flat(c_moba_k), page_table).reshape(B, past_len, MOBA_HEADS, HEAD_DIM)
    moba_k_all = jnp.concatenate([moba_past, moba_k], axis=1)
    wb = c_win_k.shape[1]
    wk_all = jnp.concatenate([c_win_k, k_win], axis=1)
    wv_all = jnp.concatenate([c_win_v, v_win], axis=1)
    win_pos = past_len - wb + jnp.arange(wb + T, dtype=jnp.int32)
    o_nsa = _sample_nsa(nsa_q, nsa_g, pos, c_k, c_v, total,
                        _paged_fetch(c_slc_k, page_table, k_slc), _paged_fetch(c_slc_v, page_table, v_slc),
                        wk_all, wv_all, win_pos)
    o_moba = _sample_moba(moba_q, pos, moba_k_all, _contiguous_fetch(moba_k_all),
                          _paged_fetch(c_moba_v, page_table, moba_v))
    mq = pr["mem_q"].astype(f32).reshape(B, T, MEM_HEADS, MEM_HEAD_DIM)
    s = jnp.einsum("bthd,bmhd->bthm", mq, c_mem_k, preferred_element_type=f32) * (MEM_HEAD_DIM ** -0.5)
    o_mem = jnp.einsum("bthm,bmhd->bthd", jax.nn.softmax(s, axis=-1), c_mem_v).reshape(B, T, MEM_W)
    merged = _merge(x.reshape(B * T, D), o_nsa.reshape(B * T, -1).astype(bf16), o_moba.reshape(B * T, -1).astype(bf16),
                    o_mem.reshape(B * T, -1).astype(bf16), lw, nsa_head_order=STD_HEAD_ORDER)
    state = (k_cmp, v_cmp, k_slc, v_slc, wk_all[:, -wb:], wv_all[:, -wb:], moba_k, moba_v)
    return merged, state


def kernel(x_prompt, x_sample, mem_prompt, cache_nsa_cmp_k, cache_nsa_cmp_v, cache_nsa_slc_k, cache_nsa_slc_v, cache_nsa_win_k, cache_nsa_win_v, cache_moba_k, cache_moba_v, cache_mem_k, cache_mem_v, page_table, norm_mix, norm_mem, w_in, w_mem_kv, cmp_w1_k, cmp_b1_k, cmp_w2_k, cmp_w1_v, cmp_b1_v, cmp_w2_v, w_nsa_out, w_moba_out, w_mem_out, w_merge_gate, b_merge_gate, w_out, norm_ffn, w_group, b_group, w_router, b_router, w_expert_gate, w_expert_up, w_expert_down, norm_final):
    depth = norm_mix.shape[0]
    xp, xs = x_prompt, x_sample
    Bp, S, D = xp.shape
    Bs, T, _ = xs.shape
    n_p = Bp * S
    p_layers, s_layers = [], []
    for l in range(depth):
        lw = dict(norm_mix=norm_mix[l], norm_mem=norm_mem[l], w_in=w_in[l], w_mem_kv=w_mem_kv[l],
                  cmp_w1_k=cmp_w1_k[l], cmp_b1_k=cmp_b1_k[l], cmp_w2_k=cmp_w2_k[l],
                  cmp_w1_v=cmp_w1_v[l], cmp_b1_v=cmp_b1_v[l], cmp_w2_v=cmp_w2_v[l],
                  w_nsa_out=w_nsa_out[l], w_moba_out=w_moba_out[l], w_mem_out=w_mem_out[l],
                  w_merge_gate=w_merge_gate[l], b_merge_gate=b_merge_gate[l], w_out=w_out[l],
                  norm_ffn=norm_ffn[l], w_group=w_group[l], b_group=b_group[l],
                  w_router=w_router[l], b_router=b_router[l], w_expert_gate=w_expert_gate[l],
                  w_expert_up=w_expert_up[l], w_expert_down=w_expert_down[l])
        (xp_mid, hp, lp), p_new = _prompt_layer(xp, mem_prompt, lw)
        (xs_mid, hs, ls), s_new = _sample_layer(xs, cache_nsa_cmp_k[l], cache_nsa_cmp_v[l], cache_nsa_slc_k[l],
                                                cache_nsa_slc_v[l], cache_nsa_win_k[l], cache_nsa_win_v[l],
                                                cache_moba_k[l], cache_moba_v[l], cache_mem_k[l], cache_mem_v[l],
                                                page_table, lw)
        picked, weight = _moe_rows([hp, hs], jnp.concatenate([lp, ls], axis=0), lw)
        last = l == depth - 1
        xp = _combine(xp_mid, picked[:n_p], weight[:n_p], norm_final, normalise=last).reshape(Bp, S, D)
        xs = _combine(xs_mid, picked[n_p:], weight[n_p:], norm_final, normalise=last).reshape(Bs, T, D)
        p_layers.append(p_new)
        s_layers.append(s_new)
    p_out = [jnp.stack(z) for z in zip(*p_layers)]
    s_out = [jnp.stack(z) for z in zip(*s_layers)]
    return (xp, xs, *p_out, *s_out)
```

```python
import functools

import numpy as np
import jax
import jax.numpy as jnp
from jax import lax
from jax.experimental import pallas as pl
from jax.experimental.pallas import tpu as pltpu

f32 = jnp.float32
bf16 = jnp.bfloat16

D_MODEL = 1024
PAGE_SIZE = 128
HEAD_DIM = 64
NSA_HEADS = 8
NSA_GROUPS = 2
NSA_REP = NSA_HEADS // NSA_GROUPS
NSA_CMP_LEN = 32
NSA_CMP_STRIDE = 16
NSA_CMP_HIDDEN = 256
NSA_SEL_BLOCK = 64
NSA_TOPN = 16
NSA_WINDOW = 512
MOBA_HEADS = 8
MOBA_BLOCK = 256
MOBA_TOPK = 3
MEM_HEADS = 4
MEM_HEAD_DIM = 128
N_BRANCHES = 3
N_GROUPS = 4
EXPERTS_PER_GROUP = 8
N_EXPERTS = N_GROUPS * EXPERTS_PER_GROUP
TOP_K_IN_GROUP = 2
EXPERT_FF = 512
ROPE_THETA = 10000.0
NORM_EPS = 1e-6

NSA_Q_W = NSA_HEADS * HEAD_DIM
NSA_KV_W = NSA_GROUPS * HEAD_DIM
NSA_GATE_W = NSA_HEADS * 3
MOBA_W = MOBA_HEADS * HEAD_DIM
MEM_W = MEM_HEADS * MEM_HEAD_DIM
PROJ_WIDTH = NSA_Q_W + NSA_GATE_W + 6 * NSA_KV_W + 3 * MOBA_W + MEM_W

LANES = 128
VMEM_LIMIT = 56 << 20
MASKED = -1e30
ROW_MAX_INIT = -1e20
FORCED = 1e30
MOE_ROWS = 256
HEAD_ROWS = 16

assert NSA_KV_W == LANES and 2 * HEAD_DIM == LANES and MEM_HEAD_DIM == LANES
NSA_HEAD_ORDER = tuple(g * NSA_REP + r for r in range(NSA_REP) for g in range(NSA_GROUPS))
STD_HEAD_ORDER = tuple(range(NSA_HEADS))


def _round_up(n, m):
    return -(-n // m) * m


def _params(*sem):
    return pltpu.CompilerParams(dimension_semantics=sem, vmem_limit_bytes=VMEM_LIMIT)


def _rms(xf, gain):
    return xf * lax.rsqrt(jnp.mean(xf * xf, axis=-1, keepdims=True) + NORM_EPS) * gain


def _dot(a, b):
    return jnp.dot(a, b, preferred_element_type=f32)


def _dot_nt(a, b):
    return lax.dot_general(a, b, (((1,), (1,)), ((), ())), preferred_element_type=f32)


def _split_hi_lo(a):
    hi = a.astype(bf16)
    lo = (a - hi.astype(f32)).astype(bf16)
    return hi, lo


def _dot_f32(a, b, dot=_dot):
    a_hi, a_lo = _split_hi_lo(a)
    b_hi, b_lo = _split_hi_lo(b)
    return dot(a_hi, b_hi) + dot(a_lo, b_hi) + dot(a_hi, b_lo)


def _mm_kernel(x_ref, w_ref, o_ref):
    o_ref[...] = _dot(x_ref[...].astype(bf16), w_ref[...]).astype(o_ref.dtype)


def _norm_mm_kernel(x_ref, g_ref, w_ref, o_ref, xn_ref):
    @pl.when(pl.program_id(1) == 0)
    def _():
        xn_ref[...] = _rms(x_ref[...], g_ref[...]).astype(bf16)

    o_ref[...] = _dot(xn_ref[...], w_ref[...]).astype(o_ref.dtype)


def _matmul(x, w, *, gain=None, out_dtype=f32, tm=512, tn=512, name="matmul"):
    M, K = x.shape
    N = w.shape[1]
    tm = min(tm, M)
    tn = max(t for t in range(LANES, min(tn, N) + 1, LANES) if N % t == 0)
    assert M % tm == 0 and N % LANES == 0, (M, N, tm, tn)
    grid = (M // tm, N // tn)
    x_spec = pl.BlockSpec((tm, K), lambda i, j: (i, 0))
    w_spec = pl.BlockSpec((K, tn), lambda i, j: (0, j))
    o_spec = pl.BlockSpec((tm, tn), lambda i, j: (i, j))
    out_shape = jax.ShapeDtypeStruct((M, N), out_dtype)
    if gain is None:
        return pl.pallas_call(_mm_kernel, grid=grid, in_specs=[x_spec, w_spec], out_specs=o_spec,
                              out_shape=out_shape, compiler_params=_params("parallel", "arbitrary"),
                              name=name)(x, w)
    return pl.pallas_call(
        _norm_mm_kernel, grid=grid,
        in_specs=[x_spec, pl.BlockSpec((1, K), lambda i, j: (0, 0)), w_spec],
        out_specs=o_spec, out_shape=out_shape, scratch_shapes=[pltpu.VMEM((tm, K), bf16)],
        compiler_params=_params("parallel", "arbitrary"), name=name)(x, gain.reshape(1, K), w)


_PROJ_SEGMENTS = (("nsa_q", 4, True), ("k_cmp", 1, True), ("v_cmp", 1, False), ("k_slc", 1, True),
                  ("v_slc", 1, False), ("k_win", 1, True), ("v_win", 1, False), ("moba_q", 4, True),
                  ("moba_k", 4, True), ("moba_v", 4, False), ("mem_q", 4, False), ("nsa_g", 1, False))
_PROJ_BLOCKS = sum(n for _, n, _ in _PROJ_SEGMENTS)


def _proj_kernel(x_ref, g_ref, w_ref, cos_ref, sin_ref, *o_refs, q_scale):
    xn = _rms(x_ref[...], g_ref[...]).astype(bf16)
    cos = cos_ref[...]
    sin = sin_ref[...]
    lane = lax.broadcasted_iota(jnp.int32, (1, LANES), 1)
    first_half = (lane & (HEAD_DIM // 2)) == 0

    def rope(a):
        partner = jnp.where(first_half, pltpu.roll(a, LANES - HEAD_DIM // 2, 1), pltpu.roll(a, HEAD_DIM // 2, 1))
        return a * cos + partner * sin

    c = 0
    for (name, n, rotary), o_ref in zip(_PROJ_SEGMENTS, o_refs):
        acc = _dot(xn, w_ref[:, c * LANES:(c + n) * LANES])
        c += n
        for r in range(n):
            a = acc[:, r * LANES:(r + 1) * LANES]
            if rotary:
                a = rope(a)
            if name == "nsa_q":
                a = a * q_scale
            if name == "nsa_g":
                a = jax.nn.sigmoid(a)
            o_ref[:, r * LANES:(r + 1) * LANES] = a.astype(o_ref.dtype)


def _proj_weight(w_in, head_order):
    D = w_in.shape[0]
    q = w_in[:, :NSA_Q_W].reshape(D, NSA_HEADS, HEAD_DIM)[:, np.array(head_order)].reshape(D, NSA_Q_W)
    gates = jnp.pad(w_in[:, NSA_Q_W:NSA_Q_W + NSA_GATE_W], ((0, 0), (0, LANES - NSA_GATE_W)))
    return jnp.concatenate([q, w_in[:, NSA_Q_W + NSA_GATE_W:], gates], axis=1).astype(bf16)


def _rope_tables(pos):
    half = HEAD_DIM // 2
    inv_freq = ROPE_THETA ** (-jnp.arange(half, dtype=f32) / half)
    ang = pos.astype(f32)[:, None] * inv_freq[None, :]
    cos, sin = jnp.cos(ang), jnp.sin(ang)
    reps = LANES // HEAD_DIM
    return jnp.tile(jnp.concatenate([cos, cos], axis=1), (1, reps)), jnp.tile(jnp.concatenate([-sin, sin], axis=1), (1, reps))


def _project(x, pos, lw, *, head_order, q_dtype, q_scale, tm=512):
    M, D = x.shape
    P = pos.shape[0]
    tm = min(tm, M, P)
    assert M % tm == 0 and P % tm == 0
    cos, sin = _rope_tables(pos)
    dtypes = dict(nsa_q=q_dtype, mem_q=bf16)
    out_shape = [jax.ShapeDtypeStruct((M, n * LANES), dtypes.get(name, f32)) for name, n, _ in _PROJ_SEGMENTS]
    out_specs = [pl.BlockSpec((tm, n * LANES), lambda i: (i, 0)) for _, n, _ in _PROJ_SEGMENTS]
    n_pos = P // tm
    outs = pl.pallas_call(
        functools.partial(_proj_kernel, q_scale=q_scale), grid=(M // tm,),
        in_specs=[pl.BlockSpec((tm, D), lambda i: (i, 0)), pl.BlockSpec((1, D), lambda i: (0, 0)),
                  pl.BlockSpec((D, _PROJ_BLOCKS * LANES), lambda i: (0, 0)),
                  pl.BlockSpec((tm, LANES), lambda i: (i % n_pos, 0)),
                  pl.BlockSpec((tm, LANES), lambda i: (i % n_pos, 0))],
        out_specs=out_specs, out_shape=out_shape, compiler_params=_params("parallel"),
        name="proj_in")(x, lw["norm_mix"].reshape(1, D), _proj_weight(lw["w_in"], head_order), cos, sin)
    return {name: o for (name, _, _), o in zip(_PROJ_SEGMENTS, outs)}


def _compress_kernel(rows_ref, nxt_ref, w1_ref, b1_ref, w2_ref, o_ref, buf_ref, *, n_chunks):
    stride = NSA_CMP_STRIDE
    n_rows = n_chunks * stride
    buf_ref[0:n_rows, :] = rows_ref[0]
    buf_ref[n_rows:n_rows + stride, :] = nxt_ref[0]
    acc = jnp.zeros((n_chunks, w1_ref.shape[-1]), f32)
    for p in range(0, NSA_CMP_LEN, 2):
        lhs = jnp.concatenate([buf_ref[pl.ds(p, n_chunks, stride=stride), :],
                               buf_ref[pl.ds(p + 1, n_chunks, stride=stride), :]], axis=1)
        acc = acc + _dot(lhs.astype(bf16), w1_ref[p // 2])
    hid = jax.nn.gelu(acc + b1_ref[...])
    o_ref[0] = _dot(hid.astype(bf16), w2_ref[...])


def _compress(rows, w1, b1, w2, *, chunks_per_step):
    B, L, W = rows.shape
    G, dh, stride = NSA_GROUPS, HEAD_DIM, NSA_CMP_STRIDE
    n_total = L // stride
    C = min(chunks_per_step, n_total)
    assert W == G * dh and L % stride == 0 and n_total % C == 0 and NSA_CMP_LEN == 2 * stride
    w1bd, b1bd, w2bd = _compress_weights(w1, b1, w2)
    n_steps = n_total // C
    last_blk = L // stride - 1
    return pl.pallas_call(
        functools.partial(_compress_kernel, n_chunks=C), grid=(B, n_steps),
        in_specs=[pl.BlockSpec((1, C * stride, W), lambda b, i: (b, i, 0)),
                  pl.BlockSpec((1, stride, W), lambda b, i: (b, jnp.minimum((i + 1) * C, last_blk), 0)),
                  pl.BlockSpec(w1bd.shape, lambda b, i: (0, 0, 0)),
                  pl.BlockSpec(b1bd.shape, lambda b, i: (0, 0)),
                  pl.BlockSpec(w2bd.shape, lambda b, i: (0, 0))],
        out_specs=pl.BlockSpec((1, C, W), lambda b, i: (b, i, 0)),
        out_shape=jax.ShapeDtypeStruct((B, n_total, W), f32),
        scratch_shapes=[pltpu.VMEM((C * stride + stride, W), f32)],
        compiler_params=_params("parallel", "arbitrary"), name="compress")(rows, rows, w1bd, b1bd, w2bd)


def _flash_tile(q, kt, vt, mask, m, l, acc, rep):
    tq, tk = mask.shape
    s = _dot_nt(q, kt.astype(bf16)).reshape(rep, tq, tk)
    s = jnp.where(mask[None], s, MASKED)
    m_new = jnp.maximum(m, jnp.max(s, axis=-1, keepdims=True))
    p = jnp.exp(s - m_new)
    alpha = jnp.exp(m - m_new)
    l = alpha * l + jnp.sum(p, axis=-1, keepdims=True)
    pv = _dot(p.reshape(rep * tq, tk).astype(bf16), vt.astype(bf16))
    acc = alpha.reshape(rep * tq, 1) * acc + pv
    return m_new, l, acc


def _softmax_tile(q, kt, vt, mask, rep):
    tq, tk = mask.shape
    s = jnp.where(mask[None], _dot_nt(q, kt).reshape(rep, tq, tk), MASKED)
    m = jnp.maximum(jnp.max(s, axis=-1, keepdims=True), ROW_MAX_INIT)
    e = jnp.exp(s - m)
    den = jnp.sum(e, axis=-1, keepdims=True)
    p = e / jnp.where(den > 0, den, 1.0)
    return p, _dot(p.reshape(rep * tq, tk).astype(bf16), vt)


def _flash_init(rep, tq):
    return (jnp.full((rep, tq, 1), ROW_MAX_INIT, f32), jnp.zeros((rep, tq, 1), f32),
            jnp.zeros((rep * tq, LANES), f32))


def _flash_finish(l, acc):
    l = l.reshape(acc.shape[0], 1)
    return acc / jnp.where(l > 0, l, 1.0)


def _topk_mask(score, k, col_index, n_real):
    rank = jnp.zeros(score.shape, jnp.int32)
    for i in range(n_real):
        si = score[:, i:i + 1]
        ahead = (si > score) | ((si == score) & (col_index > i))
        rank = rank + jnp.where(ahead, 1, 0)
    return rank < k


def _mask_to_bf16(mask):
    return jnp.where(mask, 1.0, 0.0).astype(bf16)


def _nsa_prompt_kernel(q_ref, g_ref, ck_ref, cv_ref, k_ref, v_ref, wk_ref, wv_ref,
                       imp_ref, esel_ref, o_ref, selexp_ref, *, tq, tk, n_sel):
    rep = NSA_REP
    s0 = pl.program_id(1) * tq
    qpos = s0 + lax.broadcasted_iota(jnp.int32, (tq, 1), 0)
    n_cmp = ck_ref.shape[1]
    cmp_end = lax.broadcasted_iota(jnp.int32, (1, n_cmp), 1) * NSA_CMP_STRIDE + (NSA_CMP_LEN - 1)
    cmp_mask = cmp_end <= qpos
    lane = lax.broadcasted_iota(jnp.int32, (1, LANES), 1)
    cur = jnp.right_shift(qpos, NSA_SEL_BLOCK.bit_length() - 1)
    visible = lane <= cur
    forced = (lane == 0) | (lane == cur) | (lane == cur - 1)
    ck = ck_ref[0].astype(bf16)
    cv = cv_ref[0].astype(bf16)
    groups = range(NSA_GROUPS)

    def group_query(g):
        in_group = (lane >= g * HEAD_DIM) & (lane < (g + 1) * HEAD_DIM)
        return jnp.concatenate(
            [jnp.where(in_group, q_ref[0, :, r * LANES:(r + 1) * LANES], jnp.zeros((), bf16)) for r in range(rep)],
            axis=0)

    q = [group_query(g) for g in groups]

    o_c = []
    for g in groups:
        p_c, o = _softmax_tile(q[g], ck, cv, cmp_mask, rep)
        o_c.append(o)
        p_hi, p_lo = _split_hi_lo(jnp.sum(p_c, axis=0))
        imp = _dot(p_hi, imp_ref[...]) + _dot(p_lo, imp_ref[...])
        score = jnp.where(visible, jnp.where(forced, FORCED, imp), -FORCED)
        sel = _topk_mask(score, NSA_TOPN, lane, n_sel) & visible
        selexp_ref[g] = _dot(_mask_to_bf16(sel), esel_ref[...])

    def slc_body(t, carry):
        k0 = pl.multiple_of(t * tk, tk)
        causal = (k0 + lax.broadcasted_iota(jnp.int32, (1, tk), 1)) <= qpos
        kt = k_ref[0, pl.ds(k0, tk), :].astype(bf16)
        vt = v_ref[0, pl.ds(k0, tk), :].astype(bf16)
        return tuple(_flash_tile(q[g], kt, vt, (selexp_ref[g, :, pl.ds(k0, tk)] > 0.5) & causal, *carry[g], rep)
                     for g in groups)

    slc = lax.fori_loop(0, (s0 + tq + tk - 1) // tk, slc_body, tuple(_flash_init(rep, tq) for g in groups))
    o_s = [_flash_finish(l, acc) for _, l, acc in slc]

    n_win = NSA_WINDOW + tq
    w0 = pl.multiple_of(jnp.maximum(s0 - NSA_WINDOW, 0), tq)
    kpos = w0 + lax.broadcasted_iota(jnp.int32, (1, n_win), 1)
    wmask = (kpos <= qpos) & (kpos > qpos - NSA_WINDOW)
    wk = wk_ref[0, pl.ds(w0, n_win), :].astype(bf16)
    wv = wv_ref[0, pl.ds(w0, n_win), :].astype(bf16)
    o_w = [_softmax_tile(q[g], wk, wv, wmask, rep)[1] for g in groups]

    for r in range(rep):
        halves = []
        for g in groups:
            c = 3 * (g * rep + r)
            rows = slice(r * tq, (r + 1) * tq)
            halves.append(g_ref[0, :, c:c + 1] * o_c[g][rows] + g_ref[0, :, c + 1:c + 2] * o_s[g][rows]
                          + g_ref[0, :, c + 2:c + 3] * o_w[g][rows])
        o_ref[0, :, r * LANES:(r + 1) * LANES] = jnp.where(lane < HEAD_DIM, halves[0], halves[1]).astype(o_ref.dtype)


def _nsa_static_maps(n_cmp_pad, n_sel):
    ratio_c = NSA_CMP_LEN // NSA_CMP_STRIDE
    ratio_s = NSA_SEL_BLOCK // NSA_CMP_STRIDE
    nc = n_cmp_pad - ratio_c + 1
    mat = np.zeros((n_cmp_pad, n_sel), np.float32)
    j = np.arange(n_sel)
    for mm in range(ratio_s):
        for n in range(ratio_c):
            i = ratio_s * j + mm - n
            ok = (i >= 0) & (i < nc)
            np.add.at(mat, (i[ok], j[ok]), 1.0)
    return mat


def _block_expand(n_blocks, block, total):
    k = np.arange(total)
    return (k[None, :] // block == np.arange(n_blocks)[:, None]).astype(np.float32)


def _nsa_prompt(q, gates, c_k, c_v, k_slc, v_slc, k_win, v_win, *, tq=128, tk=256):
    B, S, _ = q.shape
    n_cmp = S // NSA_CMP_STRIDE
    n_sel = max(S // NSA_SEL_BLOCK, NSA_TOPN)
    assert S % tq == 0 and S % tk == 0 and NSA_WINDOW % tq == 0 and S >= NSA_WINDOW + tq
    assert n_sel * NSA_SEL_BLOCK == S and n_sel <= LANES and c_k.shape[1] == n_cmp
    imp = jnp.asarray(np.pad(_nsa_static_maps(n_cmp, n_sel), ((0, 0), (0, LANES - n_sel))), bf16)
    esel = jnp.asarray(_block_expand(LANES, NSA_SEL_BLOCK, S), bf16)
    per_b = lambda rows: pl.BlockSpec((1, rows, LANES), lambda b, i: (b, 0, 0))
    full = lambda shape: pl.BlockSpec(shape, lambda b, i: (0,) * len(shape))
    return pl.pallas_call(
        functools.partial(_nsa_prompt_kernel, tq=tq, tk=tk, n_sel=n_sel),
        grid=(B, S // tq),
        in_specs=[pl.BlockSpec((1, tq, NSA_Q_W), lambda b, i: (b, i, 0)),
                  pl.BlockSpec((1, tq, LANES), lambda b, i: (b, i, 0)),
                  per_b(n_cmp), per_b(n_cmp), per_b(S), per_b(S), per_b(S), per_b(S),
                  full((n_cmp, LANES)), full((LANES, S))],
        out_specs=pl.BlockSpec((1, tq, NSA_Q_W), lambda b, i: (b, i, 0)),
        out_shape=jax.ShapeDtypeStruct((B, S, NSA_Q_W), bf16),
        scratch_shapes=[pltpu.VMEM((NSA_GROUPS, tq, S), f32)],
        compiler_params=_params("parallel", "arbitrary"),
        name="nsa_prompt")(q, gates, c_k, c_v, k_slc, v_slc, k_win, v_win, imp, esel)


def _moba_prompt_kernel(q_ref, k_ref, v_ref, ablk_ref, o_ref, means_ref, *, tq, n_blk):
    tk = MOBA_BLOCK
    n_pairs = q_ref.shape[-1] // LANES
    qi = pl.program_id(1)
    s0 = qi * tq
    cur = s0 // MOBA_BLOCK
    qpos = s0 + (lax.broadcasted_iota(jnp.int32, (2 * tq, 1), 0) & (tq - 1))
    lane = lax.broadcasted_iota(jnp.int32, (1, LANES), 1)
    low = lane < HEAD_DIM
    past = lane < cur

    @pl.when(qi == 0)
    def _():
        for j in range(n_pairs):
            k_hi, k_lo = _split_hi_lo(k_ref[0, :, j * LANES:(j + 1) * LANES])
            means_ref[j] = _dot(ablk_ref[...], k_hi) + _dot(ablk_ref[...], k_lo)

    pairs = range(n_pairs)
    cols = [slice(j * LANES, (j + 1) * LANES) for j in pairs]
    q, sel = [], []
    for j in pairs:
        qj = q_ref[0, :, cols[j]]
        qf = jnp.concatenate([jnp.where(low, qj, 0.0), jnp.where(low, 0.0, qj)], axis=0)
        gate = _dot_f32(qf, means_ref[j], _dot_nt)
        score = jnp.where(past, gate, -FORCED)
        picked = (_topk_mask(score, MOBA_TOPK, lane, n_blk) & past) | (lane == cur)
        sel.append(jnp.where(picked, 1.0, 0.0))
        q.append((qf * (HEAD_DIM ** -0.5)).astype(bf16))

    def body(t, carry):
        k0 = pl.multiple_of(t * tk, tk)
        causal = (k0 + lax.broadcasted_iota(jnp.int32, (1, tk), 1)) <= qpos
        out = []
        for j in pairs:
            block_on = jnp.max(jnp.where(lane == t, sel[j], 0.0), axis=1, keepdims=True) > 0.5
            out.append(_flash_tile(q[j], k_ref[0, pl.ds(k0, tk), cols[j]], v_ref[0, pl.ds(k0, tk), cols[j]],
                                   block_on & causal, *carry[j], 1))
        return tuple(out)

    state = lax.fori_loop(0, cur + 1, body, tuple(_flash_init(1, 2 * tq) for j in pairs))
    for j in pairs:
        o = _flash_finish(state[j][1], state[j][2])
        o_ref[0, :, cols[j]] = jnp.where(low, o[:tq], o[tq:]).astype(o_ref.dtype)


def _moba_prompt(q, k, v, *, tq=256):
    B, S, W = q.shape
    n_blk = S // MOBA_BLOCK
    assert S % MOBA_BLOCK == 0 and MOBA_BLOCK % tq == 0 and tq & (tq - 1) == 0
    assert MOBA_TOPK <= n_blk <= LANES and W % LANES == 0
    ablk = jnp.asarray(_block_expand(LANES, MOBA_BLOCK, S) / MOBA_BLOCK, bf16)
    per_b = pl.BlockSpec((1, S, W), lambda b, i: (b, 0, 0))
    return pl.pallas_call(
        functools.partial(_moba_prompt_kernel, tq=tq, n_blk=n_blk),
        grid=(B, S // tq),
        in_specs=[pl.BlockSpec((1, tq, W), lambda b, i: (b, i, 0)), per_b, per_b,
                  pl.BlockSpec((LANES, S), lambda b, i: (0, 0))],
        out_specs=pl.BlockSpec((1, tq, W), lambda b, i: (b, i, 0)),
        out_shape=jax.ShapeDtypeStruct((B, S, W), bf16),
        scratch_shapes=[pltpu.VMEM((W // LANES, LANES, LANES), f32)],
        compiler_params=_params("parallel", "arbitrary"),
        name="moba_prompt")(q, k, v, ablk)


def _mem_attn_kernel(q_ref, kv_ref, o_ref):
    d = MEM_HEAD_DIM
    for h in range(MEM_HEADS):
        k = kv_ref[0, :, h * d:(h + 1) * d].astype(bf16)
        v = kv_ref[0, :, MEM_W + h * d:MEM_W + (h + 1) * d].astype(bf16)
        s = _dot_nt(q_ref[0, :, h * d:(h + 1) * d], k) * (d ** -0.5)
        e = jnp.exp(s - jnp.max(s, axis=-1, keepdims=True))
        p = e / jnp.sum(e, axis=-1, keepdims=True)
        o_ref[0, :, h * d:(h + 1) * d] = _dot(p.astype(bf16), v).astype(o_ref.dtype)


def _mem_attn(q, kv, *, tq=512):
    B, T, W = q.shape
    M = kv.shape[1]
    tq = min(tq, T)
    assert T % tq == 0
    return pl.pallas_call(
        _mem_attn_kernel, grid=(B, T // tq),
        in_specs=[pl.BlockSpec((1, tq, W), lambda b, i: (b, i, 0)),
                  pl.BlockSpec((1, M, 2 * W), lambda b, i: (b, 0, 0))],
        out_specs=pl.BlockSpec((1, tq, W), lambda b, i: (b, i, 0)),
        out_shape=jax.ShapeDtypeStruct((B, T, W), bf16),
        compiler_params=_params("parallel", "arbitrary"), name="mem_attn")(q, kv)


def _merge_kernel(x_ref, gmix_ref, on_ref, om_ref, oe_ref, wg_ref, bg_ref, wn_ref, wm_ref, we_ref, wo_ref,
                  gffn_ref, wr_ref, br_ref, xmid_ref, h_ref, logit_ref):
    D = x_ref.shape[1]
    xf = x_ref[...]
    hn = _rms(xf, gmix_ref[...]).astype(bf16)
    mixed = None
    for i, (o_ref, w_ref) in enumerate(((on_ref, wn_ref), (om_ref, wm_ref), (oe_ref, we_ref))):
        gate = jax.nn.sigmoid(_dot(hn, wg_ref[:, i * D:(i + 1) * D]) + bg_ref[:, i * D:(i + 1) * D])
        term = gate * _dot(o_ref[...], w_ref[...])
        mixed = term if mixed is None else mixed + term
    xm = xf + _dot(mixed.astype(bf16), wo_ref[...])
    xmid_ref[...] = xm
    y = _rms(xm, gffn_ref[...])
    h_ref[...] = y.astype(bf16)
    logit_ref[...] = _dot_f32(y, wr_ref[...]) + br_ref[...]


def _merge(x, o_nsa, o_moba, o_mem, lw, *, nsa_head_order, tm=512):
    M, D = x.shape
    tm = min(tm, M)
    assert M % tm == 0
    n_route = N_GROUPS + N_EXPERTS
    w_route = jnp.pad(jnp.concatenate([lw["w_group"], lw["w_router"]], axis=1), ((0, 0), (0, LANES - n_route)))
    b_route = jnp.pad(jnp.concatenate([lw["b_group"], lw["b_router"]]), (0, LANES - n_route)).reshape(1, LANES)
    w_nsa = lw["w_nsa_out"].reshape(NSA_HEADS, HEAD_DIM, D)[np.array(nsa_head_order)].reshape(NSA_Q_W, D)
    row = lambda w: pl.BlockSpec((tm, w), lambda i: (i, 0))
    const = lambda a: pl.BlockSpec(a.shape, lambda i: (0,) * a.ndim)
    consts = [lw["w_merge_gate"].astype(bf16), lw["b_merge_gate"].reshape(1, -1), w_nsa.astype(bf16),
              lw["w_moba_out"].astype(bf16), lw["w_mem_out"].astype(bf16), lw["w_out"].astype(bf16),
              lw["norm_ffn"].reshape(1, D), w_route, b_route]
    gmix = lw["norm_mix"].reshape(1, D)
    return pl.pallas_call(
        _merge_kernel, grid=(M // tm,),
        in_specs=[row(D), const(gmix), row(NSA_Q_W), row(MOBA_W), row(MEM_W)] + [const(a) for a in consts],
        out_specs=[row(D), row(D), row(LANES)],
        out_shape=[jax.ShapeDtypeStruct((M, D), f32), jax.ShapeDtypeStruct((M, D), bf16),
                   jax.ShapeDtypeStruct((M, LANES), f32)],
        compiler_params=_params("parallel"), name="merge")(x, gmix, o_nsa, o_moba, o_mem, *consts)


def _expert_kernel(be_ref, x_ref, wg_ref, wu_ref, wd_ref, o_ref):
    x = x_ref[...]
    a = _dot(x, wg_ref[0])
    u = _dot(x, wu_ref[0])
    mid = (a * jax.nn.sigmoid(a)) * u
    o_ref[...] = _dot(mid.astype(bf16), wd_ref[0])


def _expert_ffn(blk_expert, xs, w_gate, w_up, w_down):
    cap, D = xs.shape
    F = w_gate.shape[-1]
    return pl.pallas_call(
        _expert_kernel,
        grid_spec=pltpu.PrefetchScalarGridSpec(
            num_scalar_prefetch=1, grid=(cap // MOE_ROWS,),
            in_specs=[pl.BlockSpec((MOE_ROWS, D), lambda i, be: (i, 0)),
                      pl.BlockSpec((1, D, F), lambda i, be: (be[i], 0, 0)),
                      pl.BlockSpec((1, D, F), lambda i, be: (be[i], 0, 0)),
                      pl.BlockSpec((1, F, D), lambda i, be: (be[i], 0, 0))],
            out_specs=pl.BlockSpec((MOE_ROWS, D), lambda i, be: (i, 0))),
        out_shape=jax.ShapeDtypeStruct((cap, D), f32),
        compiler_params=_params("arbitrary"), name="expert_ffn")(blk_expert, xs, w_gate, w_up, w_down)


def _moe_rows(h_parts, logits, lw):
    n_tok, D = logits.shape[0], h_parts[0].shape[1]
    n_route = N_GROUPS + N_EXPERTS
    g_logits = logits[:, :N_GROUPS]
    g_prob = jax.nn.softmax(g_logits, axis=-1)
    grp = jnp.argmax(g_logits, axis=-1).astype(jnp.int32)
    e_logits = logits[:, N_GROUPS:n_route].reshape(-1, N_GROUPS, EXPERTS_PER_GROUP)
    e_in = jnp.take_along_axis(e_logits, grp[:, None, None], axis=1)[:, 0]
    top_p, top_i = lax.top_k(jax.nn.softmax(e_in, axis=-1), TOP_K_IN_GROUP)
    weight = top_p / jnp.sum(top_p, axis=-1, keepdims=True) * jnp.take_along_axis(g_prob, grp[:, None], axis=1)
    expert = grp[:, None] * EXPERTS_PER_GROUP + top_i.astype(jnp.int32)
    n_assign = n_tok * TOP_K_IN_GROUP
    n_blocks = -(-n_assign // MOE_ROWS) + N_EXPERTS
    cap = n_blocks * MOE_ROWS
    flat_e = expert.reshape(-1)
    flat_t = jnp.repeat(jnp.arange(n_tok, dtype=jnp.int32), TOP_K_IN_GROUP)
    onehot = (flat_e[:, None] == jnp.arange(N_EXPERTS, dtype=jnp.int32)[None, :]).astype(jnp.int32)
    csum = jnp.cumsum(onehot, axis=0)
    rank = jnp.take_along_axis(csum, flat_e[:, None], axis=1)[:, 0] - 1
    counts = csum[-1]
    padded = (counts + MOE_ROWS - 1) // MOE_ROWS * MOE_ROWS
    pad_end = jnp.cumsum(padded)
    dest = (pad_end - padded)[flat_e] + rank
    h_pad = jnp.concatenate(list(h_parts) + [jnp.zeros((1, D), h_parts[0].dtype)], axis=0)
    slot_tok = jnp.full((cap,), n_tok, jnp.int32).at[dest].set(flat_t)
    blk_expert = jnp.minimum(
        jnp.searchsorted(pad_end, jnp.arange(n_blocks, dtype=jnp.int32) * MOE_ROWS, side="right"),
        N_EXPERTS - 1).astype(jnp.int32)
    ys = _expert_ffn(blk_expert, h_pad[slot_tok], lw["w_expert_gate"].astype(bf16),
                     lw["w_expert_up"].astype(bf16), lw["w_expert_down"].astype(bf16))
    return ys[dest].reshape(n_tok, TOP_K_IN_GROUP * D), weight


def _combine_kernel(x_ref, p_ref, w_ref, g_ref, o_ref, *, normalise):
    D = x_ref.shape[1]
    y = x_ref[...] + (w_ref[:, 0:1] * p_ref[:, :D] + w_ref[:, 1:2] * p_ref[:, D:])
    o_ref[...] = _rms(y, g_ref[...]) if normalise else y


def _combine(x, picked, weight, gain, *, normalise, tm=512):
    M, D = x.shape
    tm = min(tm, M)
    assert M % tm == 0 and TOP_K_IN_GROUP == 2
    return pl.pallas_call(
        functools.partial(_combine_kernel, normalise=normalise), grid=(M // tm,),
        in_specs=[pl.BlockSpec((tm, D), lambda i: (i, 0)), pl.BlockSpec((tm, 2 * D), lambda i: (i, 0)),
                  pl.BlockSpec((tm, 2), lambda i: (i, 0)), pl.BlockSpec((1, D), lambda i: (0, 0))],
        out_specs=pl.BlockSpec((tm, D), lambda i: (i, 0)),
        out_shape=jax.ShapeDtypeStruct((M, D), f32),
        compiler_params=_params("parallel"), name="combine")(x, picked, weight, gain.reshape(1, D))


class _PageStream:
    def __init__(self, pt_ref, pool_refs, buf_refs, sem_ref, pages_per_chunk):
        self.pt_ref, self.pools, self.bufs, self.sem = pt_ref, pool_refs, buf_refs, sem_ref
        self.P = pages_per_chunk
        self.n_chunks = pl.num_programs(1)
        self.b, self.c = pl.program_id(0), pl.program_id(1)
        self.step = self.b * self.n_chunks + self.c
        self.slot = self.step % 2

    def _copies(self, b, c, slot):
        out = []
        for k, (pool, buf) in enumerate(zip(self.pools, self.bufs)):
            for p in range(self.P):
                out.append(pltpu.make_async_copy(pool.at[self.pt_ref[b, c * self.P + p]],
                                                 buf.at[slot, pl.ds(p * PAGE_SIZE, PAGE_SIZE), :],
                                                 self.sem.at[k, slot]))
        return out

    def extra_copies(self, b, c, slot):
        return []

    def _start(self, b, c, slot):
        for cp in self._copies(b, c, slot) + self.extra_copies(b, c, slot):
            cp.start()

    def advance(self):
        @pl.when(self.step == 0)
        def _():
            self._start(self.b, self.c, self.slot)

        last_c = self.c == self.n_chunks - 1
        nb = jnp.where(last_c, self.b + 1, self.b)
        nc = jnp.where(last_c, 0, self.c + 1)

        @pl.when(self.step + 1 < pl.num_programs(0) * self.n_chunks)
        def _():
            self._start(nb, nc, 1 - self.slot)

        for cp in self._copies(self.b, self.c, self.slot) + self.extra_copies(self.b, self.c, self.slot):
            cp.wait()
        return self.slot


class _CompressStream(_PageStream):
    def extra_copies(self, b, c, slot):
        n_pages = self.pt_ref.shape[1]
        nxt = jnp.minimum((c + 1) * self.P, n_pages - 1)
        return [pltpu.make_async_copy(self.pools[0].at[self.pt_ref[b, nxt], pl.ds(0, NSA_CMP_STRIDE), :],
                                      self.bufs[0].at[slot, pl.ds(self.P * PAGE_SIZE, NSA_CMP_STRIDE), :],
                                      self.sem.at[1, slot])]


def _compress_paged_kernel(pt_ref, pool_ref, w1_ref, b1_ref, w2_ref, o_ref, buf_ref, sem_ref, *, pages):
    stride = NSA_CMP_STRIDE
    n_chunks = pages * PAGE_SIZE // stride
    slot = _CompressStream(pt_ref, [pool_ref], [buf_ref], sem_ref, pages).advance()
    acc = jnp.zeros((n_chunks, w1_ref.shape[-1]), f32)
    for p in range(0, NSA_CMP_LEN, 2):
        lhs = jnp.concatenate([buf_ref[slot, pl.ds(p, n_chunks, stride=stride), :],
                               buf_ref[slot, pl.ds(p + 1, n_chunks, stride=stride), :]], axis=1)
        acc = acc + _dot(lhs.astype(bf16), w1_ref[p // 2])
    hid = jax.nn.gelu(acc + b1_ref[...])
    o_ref[0] = _dot(hid.astype(bf16), w2_ref[...])


def _compress_weights(w1, b1, w2):
    G, dh = NSA_GROUPS, HEAD_DIM
    hidden = w1.shape[1]
    eye = jnp.eye(G, dtype=f32)
    w1p = w1.reshape(NSA_CMP_LEN, dh, hidden)
    w1bd = jnp.einsum("gh,pdn->pgdhn", eye, w1p).reshape(NSA_CMP_LEN // 2, 2 * G * dh, G * hidden).astype(bf16)
    w2bd = jnp.einsum("gh,nd->gnhd", eye, w2).reshape(G * hidden, G * dh).astype(bf16)
    return w1bd, jnp.tile(b1, G).reshape(1, G * hidden), w2bd


def _compress_paged(pool, page_table, w1, b1, w2, *, pages=32):
    B, n_pages = page_table.shape
    W = pool.shape[-1]
    assert n_pages % pages == 0 and PAGE_SIZE % NSA_CMP_STRIDE == 0 and NSA_CMP_LEN == 2 * NSA_CMP_STRIDE
    w1bd, b1bd, w2bd = _compress_weights(w1, b1, w2)
    C = pages * PAGE_SIZE // NSA_CMP_STRIDE
    n_total = n_pages * PAGE_SIZE // NSA_CMP_STRIDE
    const = lambda a: pl.BlockSpec(a.shape, lambda b, c, pt: (0,) * a.ndim)
    return pl.pallas_call(
        functools.partial(_compress_paged_kernel, pages=pages),
        grid_spec=pltpu.PrefetchScalarGridSpec(
            num_scalar_prefetch=1, grid=(B, n_pages // pages),
            in_specs=[pl.BlockSpec(memory_space=pl.ANY), const(w1bd), const(b1bd), const(w2bd)],
            out_specs=pl.BlockSpec((1, C, W), lambda b, c, pt: (b, c, 0)),
            scratch_shapes=[pltpu.VMEM((2, pages * PAGE_SIZE + NSA_CMP_STRIDE, W), f32),
                            pltpu.SemaphoreType.DMA((2, 2))]),
        out_shape=jax.ShapeDtypeStruct((B, n_total, W), f32),
        compiler_params=_params("arbitrary", "arbitrary"), name="compress_paged")(page_table, pool, w1bd, b1bd, w2bd)


def _rank_rows(score, k, n_real):
    row = lax.broadcasted_iota(jnp.int32, (score.shape[0], 1), 0)
    rank = jnp.zeros(score.shape, jnp.int32)
    for i in range(n_real):
        si = score[i:i + 1, :]
        ahead = (si > score) | ((si == score) & (row > i))
        rank = rank + jnp.where(ahead, 1, 0)
    return rank < k


def _expand_lanes(a, e_ref):
    hi, lo = _split_hi_lo(a)
    return _dot(hi, e_ref[...]) + _dot(lo, e_ref[...])


def _merge_partials(sel, m, l, acc, e_ref):
    mx = jnp.max(jnp.where(sel, m, MASKED), axis=0, keepdims=True)
    w = jnp.where(sel, jnp.exp(m - mx), 0.0)
    den = jnp.sum(w * l, axis=0, keepdims=True)
    num = jnp.sum(_expand_lanes(w, e_ref) * acc, axis=0, keepdims=True)
    den_w = _expand_lanes(jnp.broadcast_to(den, (HEAD_ROWS, LANES)), e_ref)[0:1]
    return num / den_w


def _block_partial(s, v, e_ref):
    m = jnp.max(s, axis=0, keepdims=True)
    p = jnp.exp(s - m)
    l = jnp.sum(p, axis=0, keepdims=True)
    acc = jnp.sum(_dot(p.astype(bf16), e_ref[...]) * v, axis=0, keepdims=True)
    return m, l, acc


def _moba_decode_kernel(pt_ref, qs_ref, qf_ref, new_ref, kpool_ref, vpool_ref, e_ref, o_ref,
                        kbuf_ref, vbuf_ref, sem_ref, m_ref, l_ref, acc_ref, means_ref, *, pages, n_blk):
    blocks_per_chunk = pages * PAGE_SIZE // MOBA_BLOCK
    stream = _PageStream(pt_ref, [kpool_ref, vpool_ref], [kbuf_ref, vbuf_ref], sem_ref, pages)
    slot = stream.advance()
    c = stream.c
    qs = qs_ref[0]
    for i in range(blocks_per_chunk):
        rows = pl.ds(i * MOBA_BLOCK, MOBA_BLOCK)
        kb = kbuf_ref[slot, rows, :]
        j = c * blocks_per_chunk + i
        means_ref[pl.ds(j, 1), :] = jnp.sum(kb, axis=0, keepdims=True) * (1.0 / MOBA_BLOCK)
        m, l, acc = _block_partial(_dot_nt(kb.astype(bf16), qs), vbuf_ref[slot, rows, :], e_ref)
        m_ref[pl.ds(j, 1), :] = m
        l_ref[pl.ds(j, 1), :] = l
        acc_ref[pl.ds(j, 1), :] = acc

    @pl.when(c == stream.n_chunks - 1)
    def _():
        n_rows, W = acc_ref.shape
        tail = n_rows - n_blk
        k_new = jnp.broadcast_to(new_ref[0, 0:1, :], (HEAD_ROWS, W)).astype(bf16)
        s_new = _dot_nt(k_new, qs)[0:1]
        m_ref[pl.ds(n_blk, tail), :] = jnp.broadcast_to(s_new, (tail, LANES))
        l_ref[pl.ds(n_blk, tail), :] = jnp.ones((tail, LANES), f32)
        acc_ref[pl.ds(n_blk, tail), :] = jnp.broadcast_to(new_ref[0, 1:2, :], (tail, W))
        means_ref[pl.ds(n_blk, tail), :] = jnp.zeros((tail, W), f32)
        gate = _dot_f32(means_ref[...], qf_ref[0], _dot_nt)
        row = lax.broadcasted_iota(jnp.int32, (n_rows, 1), 0)
        sel = (_rank_rows(gate, MOBA_TOPK, n_blk) & (row < n_blk)) | (row == n_blk)
        o_ref[0] = jnp.broadcast_to(_merge_partials(sel, m_ref[...], l_ref[...], acc_ref[...], e_ref), o_ref.shape[1:])


def _head_rows(q, n_heads, dtype):
    B = q.shape[0]
    dh = q.shape[1] // n_heads
    eye = jnp.eye(n_heads, dtype=q.dtype)
    rows = jnp.einsum("bhd,hg->bhgd", q.reshape(B, n_heads, dh), eye).reshape(B, n_heads, n_heads * dh)
    return jnp.pad(rows, ((0, 0), (0, LANES - n_heads), (0, 0))).astype(dtype)


def _head_expand(n_heads, dh):
    e = np.zeros((LANES, n_heads * dh), np.float32)
    for h in range(n_heads):
        e[h, h * dh:(h + 1) * dh] = 1.0
    return e


def _moba_decode(q, k_new, v_new, k_pool, v_pool, page_table, *, pages=8):
    B, W = q.shape
    n_pages = page_table.shape[1]
    past = n_pages * PAGE_SIZE
    n_blk = past // MOBA_BLOCK
    assert past % MOBA_BLOCK == 0 and n_pages % pages == 0 and (pages * PAGE_SIZE) % MOBA_BLOCK == 0
    assert MOBA_TOPK <= n_blk
    qf = _head_rows(q, MOBA_HEADS, f32)
    qs = _head_rows(q * (HEAD_DIM ** -0.5), MOBA_HEADS, bf16)
    new = jnp.pad(jnp.stack([k_new, v_new], axis=1), ((0, 0), (0, 6), (0, 0)))
    e = jnp.asarray(_head_expand(MOBA_HEADS, HEAD_DIM), bf16)
    n_rows = _round_up(n_blk + 1, 8)
    per_b = lambda rows, dt=None: pl.BlockSpec((1, rows, W), lambda b, c, pt: (b, 0, 0))
    out = pl.pallas_call(
        functools.partial(_moba_decode_kernel, pages=pages, n_blk=n_blk),
        grid_spec=pltpu.PrefetchScalarGridSpec(
            num_scalar_prefetch=1, grid=(B, n_pages // pages),
            in_specs=[per_b(LANES), per_b(LANES), per_b(8), pl.BlockSpec(memory_space=pl.ANY),
                      pl.BlockSpec(memory_space=pl.ANY), pl.BlockSpec(e.shape, lambda b, c, pt: (0, 0))],
            out_specs=per_b(8),
            scratch_shapes=[pltpu.VMEM((2, pages * PAGE_SIZE, W), f32), pltpu.VMEM((2, pages * PAGE_SIZE, W), f32),
                            pltpu.SemaphoreType.DMA((2, 2)),
                            pltpu.VMEM((n_rows, LANES), f32), pltpu.VMEM((n_rows, LANES), f32),
                            pltpu.VMEM((n_rows, W), f32), pltpu.VMEM((n_rows, W), f32)]),
        out_shape=jax.ShapeDtypeStruct((B, 8, W), f32),
        compiler_params=_params("arbitrary", "arbitrary"), name="moba_decode")(
            page_table, qs, qf, new, k_pool, v_pool, e)
    return out[:, 0]


def _nsa_decode_kernel(pt_ref, qn_ref, g_ref, new_ref, ck_ref, cv_ref, wk_ref, wv_ref, kpool_ref, vpool_ref,
                       imp_ref, e_ref, eg_ref, eye_ref, o_ref,
                       kbuf_ref, vbuf_ref, sem_ref, m_ref, l_ref, acc_ref, selT_ref, oc_ref, ow_ref,
                       *, pages, past, n_sel):
    rep = NSA_REP
    blocks_per_page = PAGE_SIZE // NSA_SEL_BLOCK
    stream = _PageStream(pt_ref, [kpool_ref, vpool_ref], [kbuf_ref, vbuf_ref], sem_ref, pages)
    slot = stream.advance()
    c = stream.c
    qn = qn_ref[0].astype(bf16)
    q8 = qn_ref[0, 0:HEAD_ROWS, :].astype(bf16)
    qpos = past
    n_rows = acc_ref.shape[0]
    lane = lax.broadcasted_iota(jnp.int32, (1, LANES), 1)
    tile4 = lambda a: jnp.concatenate([a] * rep, axis=1)

    @pl.when(c == 0)
    def _():
        n_cmp = ck_ref.shape[1]
        cmp_end = lax.broadcasted_iota(jnp.int32, (1, n_cmp), 1) * NSA_CMP_STRIDE + (NSA_CMP_LEN - 1)
        s = jnp.where(cmp_end <= qpos, _dot_nt(q8, ck_ref[0].astype(bf16)), MASKED)
        m = jnp.maximum(jnp.max(s, axis=-1, keepdims=True), ROW_MAX_INIT)
        e = jnp.exp(s - m)
        den = jnp.sum(e, axis=-1, keepdims=True)
        p_c = e / jnp.where(den > 0, den, 1.0)
        oc_ref[...] = _dot(p_c.astype(bf16), cv_ref[0].astype(bf16))
        row8 = lax.broadcasted_iota(jnp.int32, (HEAD_ROWS, 1), 0)
        p_grp = jnp.where(row8 < rep, jnp.sum(p_c[0:rep], axis=0, keepdims=True),
                          jnp.sum(p_c[rep:2 * rep], axis=0, keepdims=True))
        p_hi, p_lo = _split_hi_lo(p_grp)
        imp = _dot(p_hi, imp_ref[...]) + _dot(p_lo, imp_ref[...])
        n_sel_pad = imp.shape[1]
        jblk = lax.broadcasted_iota(jnp.int32, (1, n_sel_pad), 1)
        cur = qpos // NSA_SEL_BLOCK
        visible = jblk <= cur
        forced = (jblk == 0) | (jblk == cur) | (jblk == cur - 1)
        score = jnp.where(visible, jnp.where(forced, FORCED, imp), -FORCED)
        sel = _topk_mask(score, NSA_TOPN, jblk, n_sel) & visible
        sel_rows = jnp.concatenate([jnp.where(sel, 1.0, 0.0), jnp.zeros((LANES - HEAD_ROWS, n_sel_pad), f32)], axis=0)
        selT_ref[...] = _dot_nt(eye_ref[...], sel_rows.astype(bf16))
        n_w = wk_ref.shape[1]
        kp = (past - n_w) + lax.broadcasted_iota(jnp.int32, (1, n_w), 1)
        wmask = (kp <= qpos) & (kp > qpos - NSA_WINDOW)
        kw_new = new_ref[0, 2:3, :].astype(bf16).astype(f32)
        vw_new = new_ref[0, 3:4, :].astype(bf16).astype(f32)
        s_n = jnp.sum(q8.astype(f32) * kw_new, axis=-1, keepdims=True)
        s_w = jnp.where(wmask, _dot_nt(q8, wk_ref[0].astype(bf16)), MASKED)
        m = jnp.maximum(jnp.max(s_w, axis=-1, keepdims=True), s_n)
        e = jnp.exp(s_w - m)
        e_n = jnp.exp(s_n - m)
        den = jnp.sum(e, axis=-1, keepdims=True) + e_n
        ow_ref[...] = (_dot(e.astype(bf16), wv_ref[0].astype(bf16)) + e_n.astype(bf16).astype(f32) * vw_new) / den
        tail = n_rows - (n_sel - 1)
        k_new = jnp.broadcast_to(new_ref[0, 0:1, :], (HEAD_ROWS, LANES)).astype(bf16)
        m_ref[pl.ds(n_sel - 1, tail), :] = jnp.broadcast_to(_dot_nt(k_new, qn)[0:1], (tail, LANES))
        l_ref[pl.ds(n_sel - 1, tail), :] = jnp.ones((tail, LANES), f32)
        acc_ref[pl.ds(n_sel - 1, tail), :] = jnp.broadcast_to(tile4(new_ref[0, 1:2, :]), (tail, acc_ref.shape[1]))

    def page_body(pg, carry):
        r0 = pl.multiple_of(pg * PAGE_SIZE, PAGE_SIZE)
        s = _dot_nt(kbuf_ref[slot, pl.ds(r0, PAGE_SIZE), :].astype(bf16), qn)
        vp = vbuf_ref[slot, pl.ds(r0, PAGE_SIZE), :]
        for i in range(blocks_per_page):
            rows = slice(i * NSA_SEL_BLOCK, (i + 1) * NSA_SEL_BLOCK)
            m, l, acc = _block_partial(s[rows], tile4(vp[rows]), e_ref)
            j = (c * pages + pg) * blocks_per_page + i
            m_ref[pl.ds(j, 1), :] = m
            l_ref[pl.ds(j, 1), :] = l
            acc_ref[pl.ds(j, 1), :] = acc
        return carry

    lax.fori_loop(0, pages, page_body, 0, unroll=4)

    @pl.when(c == stream.n_chunks - 1)
    def _():
        o_s = _merge_partials(selT_ref[0:n_rows, :] > 0.5, m_ref[...], l_ref[...], acc_ref[...], e_ref)
        wide = lambda o8: jnp.concatenate(
            [jnp.where(lane < HEAD_DIM, o8[r:r + 1], o8[rep + r:rep + r + 1]) for r in range(rep)], axis=1)
        gate = [_expand_lanes(g_ref[0], eg_ref.at[i])[0:1] for i in range(3)]
        o = gate[0] * wide(oc_ref[...]) + gate[1] * o_s + gate[2] * wide(ow_ref[...])
        o_ref[0] = jnp.broadcast_to(o, o_ref.shape[1:])


def _nsa_decode(q, gates, new_rows, c_k, c_v, win_k, win_v, k_pool, v_pool, page_table, *, pages=32):
    B = q.shape[0]
    G, R, d = NSA_GROUPS, NSA_REP, HEAD_DIM
    n_pages = page_table.shape[1]
    past = n_pages * PAGE_SIZE
    n_cmp = c_k.shape[1]
    n_sel = max(-(-(past + 1) // NSA_SEL_BLOCK), NSA_TOPN)
    n_sel_pad = _round_up(n_sel, LANES)
    n_rows = _round_up(n_sel, 8)
    assert past % NSA_SEL_BLOCK == 0 and n_pages % pages == 0 and past >= win_k.shape[1] and G == 2
    assert n_cmp == past // NSA_CMP_STRIDE and n_sel - 1 == past // NSA_SEL_BLOCK
    qg = jnp.einsum("bgrd,gk->bgrkd", q.reshape(B, G, R, d) * (d ** -0.5), jnp.eye(G, dtype=f32))
    qn = jnp.pad(qg.reshape(B, NSA_HEADS, LANES), ((0, 0), (0, LANES - NSA_HEADS), (0, 0)))
    g8 = jnp.broadcast_to(gates[:, None, :], (B, HEAD_ROWS, LANES))
    new8 = jnp.pad(new_rows, ((0, 0), (0, 8 - new_rows.shape[1]), (0, 0)))
    imp = np.zeros((n_cmp, n_sel_pad), np.float32)
    imp[:, :n_sel] = _nsa_static_maps(n_cmp, n_sel)
    e_np = np.zeros((LANES, NSA_Q_W), np.float32)
    eg_np = np.zeros((3, LANES, NSA_Q_W), np.float32)
    for g in range(G):
        for r in range(R):
            h = g * R + r
            cols = slice(r * LANES + g * d, r * LANES + (g + 1) * d)
            e_np[h, cols] = 1.0
            for i in range(3):
                eg_np[i, 3 * h + i, cols] = 1.0
    consts = [jnp.asarray(imp, bf16), jnp.asarray(e_np, bf16), jnp.asarray(eg_np, bf16),
              jnp.asarray(np.eye(n_sel_pad, dtype=np.float32), bf16)]
    per_b = lambda a: pl.BlockSpec((1,) + a.shape[1:], lambda b, c, pt: (b,) + (0,) * (a.ndim - 1))
    const = lambda a: pl.BlockSpec(a.shape, lambda b, c, pt: (0,) * a.ndim)
    anyspec = pl.BlockSpec(memory_space=pl.ANY)
    per_b_in = [qn, g8, new8, c_k, c_v, win_k, win_v]
    out = pl.pallas_call(
        functools.partial(_nsa_decode_kernel, pages=pages, past=past, n_sel=n_sel),
        grid_spec=pltpu.PrefetchScalarGridSpec(
            num_scalar_prefetch=1, grid=(B, n_pages // pages),
            in_specs=[per_b(a) for a in per_b_in] + [anyspec, anyspec] + [const(a) for a in consts],
            out_specs=pl.BlockSpec((1, 8, NSA_Q_W), lambda b, c, pt: (b, 0, 0)),
            scratch_shapes=[pltpu.VMEM((2, pages * PAGE_SIZE, LANES), f32), pltpu.VMEM((2, pages * PAGE_SIZE, LANES), f32),
                            pltpu.SemaphoreType.DMA((2, 2)),
                            pltpu.VMEM((n_rows, LANES), f32), pltpu.VMEM((n_rows, LANES), f32),
                            pltpu.VMEM((n_rows, NSA_Q_W), f32), pltpu.VMEM((n_sel_pad, LANES), f32),
                            pltpu.VMEM((HEAD_ROWS, LANES), f32), pltpu.VMEM((HEAD_ROWS, LANES), f32)]),
        out_shape=jax.ShapeDtypeStruct((B, 8, NSA_Q_W), f32),
        compiler_params=_params("arbitrary", "arbitrary"), name="nsa_decode")(
            page_table, *per_b_in, k_pool, v_pool, *consts)
    return out[:, 0]


def _mem_decode_kernel(q_ref, k_ref, v_ref, o_ref):
    d = MEM_HEAD_DIM
    s = _dot_nt(q_ref[0].astype(bf16), k_ref[0].astype(bf16)) * (d ** -0.5)
    e = jnp.exp(s - jnp.max(s, axis=-1, keepdims=True))
    p = e / jnp.sum(e, axis=-1, keepdims=True)
    o8 = _dot(p.astype(bf16), v_ref[0].astype(bf16))
    own = jnp.right_shift(lax.broadcasted_iota(jnp.int32, (1, o8.shape[1]), 1), d.bit_length() - 1) == \
        lax.broadcasted_iota(jnp.int32, (HEAD_ROWS, 1), 0)
    o_ref[0] = jnp.broadcast_to(jnp.sum(jnp.where(own, o8, 0.0), axis=0, keepdims=True), o_ref.shape[1:])


def _mem_decode(q, mk, mv):
    B, W = q.shape
    M = mk.shape[1]
    q8 = _head_rows(q.astype(f32), MEM_HEADS, f32)[:, :HEAD_ROWS]
    out = pl.pallas_call(
        _mem_decode_kernel, grid=(B,),
        in_specs=[pl.BlockSpec((1, HEAD_ROWS, W), lambda b: (b, 0, 0)), pl.BlockSpec((1, M, W), lambda b: (b, 0, 0)),
                  pl.BlockSpec((1, M, W), lambda b: (b, 0, 0))],
        out_specs=pl.BlockSpec((1, 8, W), lambda b: (b, 0, 0)),
        out_shape=jax.ShapeDtypeStruct((B, 8, W), f32),
        compiler_params=_params("parallel"), name="mem_decode")(q8, mk, mv)
    return out[:, 0]


def _prompt_layer(x, mem, lw):
    B, S, D = x.shape
    pos = jnp.arange(S, dtype=jnp.int32)
    pr = _project(x.reshape(B * S, D), pos, lw, head_order=NSA_HEAD_ORDER, q_dtype=bf16, q_scale=HEAD_DIM ** -0.5)
    seq = lambda name: pr[name].reshape(B, S, -1)
    Bm, M, _ = mem.shape
    kv = _matmul(mem.reshape(Bm * M, D), lw["w_mem_kv"].astype(bf16), gain=lw["norm_mem"], name="mem_kv")
    kv = kv.reshape(Bm, M, 2 * MEM_W)
    c_k = _compress(seq("k_cmp"), lw["cmp_w1_k"], lw["cmp_b1_k"], lw["cmp_w2_k"], chunks_per_step=128)
    c_v = _compress(seq("v_cmp"), lw["cmp_w1_v"], lw["cmp_b1_v"], lw["cmp_w2_v"], chunks_per_step=128)
    o_nsa = _nsa_prompt(seq("nsa_q"), seq("nsa_g"), c_k, c_v, seq("k_slc"), seq("v_slc"), seq("k_win"), seq("v_win"))
    o_moba = _moba_prompt(seq("moba_q"), seq("moba_k"), seq("moba_v"))
    o_mem = _mem_attn(seq("mem_q"), kv)
    merged = _merge(x.reshape(B * S, D), o_nsa.reshape(B * S, -1), o_moba.reshape(B * S, -1),
                    o_mem.reshape(B * S, -1), lw, nsa_head_order=NSA_HEAD_ORDER)
    wb = min(NSA_WINDOW, S)
    heads = lambda name, n, d: pr[name].reshape(B, S, n, d)
    state = (heads("k_cmp", NSA_GROUPS, HEAD_DIM), heads("v_cmp", NSA_GROUPS, HEAD_DIM),
             heads("k_slc", NSA_GROUPS, HEAD_DIM), heads("v_slc", NSA_GROUPS, HEAD_DIM),
             heads("k_win", NSA_GROUPS, HEAD_DIM)[:, S - wb:], heads("v_win", NSA_GROUPS, HEAD_DIM)[:, S - wb:],
             heads("moba_k", MOBA_HEADS, HEAD_DIM), heads("moba_v", MOBA_HEADS, HEAD_DIM),
             kv[:, :, :MEM_W].reshape(Bm, M, MEM_HEADS, MEM_HEAD_DIM),
             kv[:, :, MEM_W:].reshape(Bm, M, MEM_HEADS, MEM_HEAD_DIM))
    return merged, state


def _sample_layer(x, c_cmp_k, c_cmp_v, c_slc_k, c_slc_v, c_win_k, c_win_v,
                  c_moba_k, c_moba_v, c_mem_k, c_mem_v, page_table, lw):
    B, T, D = x.shape
    past_len = page_table.shape[1] * PAGE_SIZE
    assert T == 1 and past_len % NSA_CMP_STRIDE == 0
    pos = jnp.full((B,), past_len, jnp.int32)
    pr = _project(x.reshape(B, D), pos, lw, head_order=STD_HEAD_ORDER, q_dtype=f32, q_scale=1.0)
    flat = lambda a: a.reshape(a.shape[0], a.shape[1], -1)
    c_k = _compress_paged(flat(c_cmp_k), page_table, lw["cmp_w1_k"], lw["cmp_b1_k"], lw["cmp_w2_k"])
    c_v = _compress_paged(flat(c_cmp_v), page_table, lw["cmp_w1_v"], lw["cmp_b1_v"], lw["cmp_w2_v"])
    new_rows = jnp.stack([pr["k_slc"], pr["v_slc"], pr["k_win"], pr["v_win"]], axis=1)
    o_nsa = _nsa_decode(pr["nsa_q"], pr["nsa_g"], new_rows, c_k, c_v, flat(c_win_k), flat(c_win_v),
                        flat(c_slc_k), flat(c_slc_v), page_table)
    o_moba = _moba_decode(pr["moba_q"], pr["moba_k"], pr["moba_v"], flat(c_moba_k), flat(c_moba_v), page_table)
    o_mem = _mem_decode(pr["mem_q"], flat(c_mem_k), flat(c_mem_v))
    merged = _merge(x.reshape(B, D), o_nsa.astype(bf16), o_moba.astype(bf16), o_mem.astype(bf16), lw,
                    nsa_head_order=NSA_HEAD_ORDER)
    heads = lambda name, n, d: pr[name].reshape(B, T, n, d)
    wb = c_win_k.shape[1]
    roll_in = lambda cache, name: jnp.concatenate([cache, heads(name, NSA_GROUPS, HEAD_DIM)], axis=1)[:, -wb:]
    state = (heads("k_cmp", NSA_GROUPS, HEAD_DIM), heads("v_cmp", NSA_GROUPS, HEAD_DIM),
             heads("k_slc", NSA_GROUPS, HEAD_DIM), heads("v_slc", NSA_GROUPS, HEAD_DIM),
             roll_in(c_win_k, "k_win"), roll_in(c_win_v, "v_win"),
             heads("moba_k", MOBA_HEADS, HEAD_DIM), heads("moba_v", MOBA_HEADS, HEAD_DIM))
    return merged, state


def kernel(x_prompt, x_sample, mem_prompt, cache_nsa_cmp_k, cache_nsa_cmp_v, cache_nsa_slc_k, cache_nsa_slc_v, cache_nsa_win_k, cache_nsa_win_v, cache_moba_k, cache_moba_v, cache_mem_k, cache_mem_v, page_table, norm_mix, norm_mem, w_in, w_mem_kv, cmp_w1_k, cmp_b1_k, cmp_w2_k, cmp_w1_v, cmp_b1_v, cmp_w2_v, w_nsa_out, w_moba_out, w_mem_out, w_merge_gate, b_merge_gate, w_out, norm_ffn, w_group, b_group, w_router, b_router, w_expert_gate, w_expert_up, w_expert_down, norm_final):
    depth = norm_mix.shape[0]
    xp, xs = x_prompt, x_sample
    Bp, S, D = xp.shape
    Bs, T, _ = xs.shape
    n_p = Bp * S
    p_layers, s_layers = [], []
    for l in range(depth):
        lw = dict(norm_mix=norm_mix[l], norm_mem=norm_mem[l], w_in=w_in[l], w_mem_kv=w_mem_kv[l],
                  cmp_w1_k=cmp_w1_k[l], cmp_b1_k=cmp_b1_k[l], cmp_w2_k=cmp_w2_k[l],
                  cmp_w1_v=cmp_w1_v[l], cmp_b1_v=cmp_b1_v[l], cmp_w2_v=cmp_w2_v[l],
                  w_nsa_out=w_nsa_out[l], w_moba_out=w_moba_out[l], w_mem_out=w_mem_out[l],
                  w_merge_gate=w_merge_gate[l], b_merge_gate=b_merge_gate[l], w_out=w_out[l],
                  norm_ffn=norm_ffn[l], w_group=w_group[l], b_group=b_group[l],
                  w_router=w_router[l], b_router=b_router[l], w_expert_gate=w_expert_gate[l],
                  w_expert_up=w_expert_up[l], w_expert_down=w_expert_down[l])
        (xp_mid, hp, lp), p_new = _prompt_layer(xp, mem_prompt, lw)
        (xs_mid, hs, ls), s_new = _sample_layer(xs, cache_nsa_cmp_k[l], cache_nsa_cmp_v[l], cache_nsa_slc_k[l],
                                                cache_nsa_slc_v[l], cache_nsa_win_k[l], cache_nsa_win_v[l],
                                                cache_moba_k[l], cache_moba_v[l], cache_mem_k[l], cache_mem_v[l],
                                                page_table, lw)
        picked, weight = _moe_rows([hp, hs], jnp.concatenate([lp, ls], axis=0), lw)
        last = l == depth - 1
        xp = _combine(xp_mid, picked[:n_p], weight[:n_p], norm_final, normalise=last).reshape(Bp, S, D)
        xs = _combine(xs_mid, picked[n_p:], weight[n_p:], norm_final, normalise=last).reshape(Bs, T, D)
        p_layers.append(p_new)
        s_layers.append(s_new)
    p_out = [jnp.stack(z) for z in zip(*p_layers)]
    s_out = [jnp.stack(z) for z in zip(*s_layers)]
    return (xp, xs, *p_out, *s_out)
```

```python
import functools

import numpy as np
import jax
import jax.numpy as jnp
from jax import lax
from jax.experimental import pallas as pl
from jax.experimental.pallas import tpu as pltpu

f32 = jnp.float32
bf16 = jnp.bfloat16

D_MODEL = 1024
PAGE_SIZE = 128
HEAD_DIM = 64
NSA_HEADS = 8
NSA_GROUPS = 2
NSA_REP = NSA_HEADS // NSA_GROUPS
NSA_CMP_LEN = 32
NSA_CMP_STRIDE = 16
NSA_CMP_HIDDEN = 256
NSA_SEL_BLOCK = 64
NSA_TOPN = 16
NSA_WINDOW = 512
MOBA_HEADS = 8
MOBA_BLOCK = 256
MOBA_TOPK = 3
MEM_HEADS = 4
MEM_HEAD_DIM = 128
N_BRANCHES = 3
N_GROUPS = 4
EXPERTS_PER_GROUP = 8
N_EXPERTS = N_GROUPS * EXPERTS_PER_GROUP
TOP_K_IN_GROUP = 2
EXPERT_FF = 512
ROPE_THETA = 10000.0
NORM_EPS = 1e-6

NSA_Q_W = NSA_HEADS * HEAD_DIM
NSA_KV_W = NSA_GROUPS * HEAD_DIM
NSA_GATE_W = NSA_HEADS * 3
MOBA_W = MOBA_HEADS * HEAD_DIM
MEM_W = MEM_HEADS * MEM_HEAD_DIM
PROJ_WIDTH = NSA_Q_W + NSA_GATE_W + 6 * NSA_KV_W + 3 * MOBA_W + MEM_W

LANES = 128
VMEM_LIMIT = 56 << 20
MASKED = -1e30
ROW_MAX_INIT = -1e20
FORCED = 1e30
MOE_ROWS = 256
HEAD_ROWS = 16

assert NSA_KV_W == LANES and 2 * HEAD_DIM == LANES and MEM_HEAD_DIM == LANES
NSA_HEAD_ORDER = tuple(g * NSA_REP + r for r in range(NSA_REP) for g in range(NSA_GROUPS))
STD_HEAD_ORDER = tuple(range(NSA_HEADS))


def _round_up(n, m):
    return -(-n // m) * m


def _params(*sem):
    return pltpu.CompilerParams(dimension_semantics=sem, vmem_limit_bytes=VMEM_LIMIT)


def _rms(xf, gain):
    return xf * lax.rsqrt(jnp.mean(xf * xf, axis=-1, keepdims=True) + NORM_EPS) * gain


def _dot(a, b):
    return jnp.dot(a, b, preferred_element_type=f32)


def _dot_nt(a, b):
    return lax.dot_general(a, b, (((1,), (1,)), ((), ())), preferred_element_type=f32)


def _split_hi_lo(a):
    hi = a.astype(bf16)
    lo = (a - hi.astype(f32)).astype(bf16)
    return hi, lo


def _mm_kernel(x_ref, w_ref, o_ref):
    o_ref[...] = _dot(x_ref[...].astype(bf16), w_ref[...]).astype(o_ref.dtype)


def _norm_mm_kernel(x_ref, g_ref, w_ref, o_ref, xn_ref):
    @pl.when(pl.program_id(1) == 0)
    def _():
        xn_ref[...] = _rms(x_ref[...], g_ref[...]).astype(bf16)

    o_ref[...] = _dot(xn_ref[...], w_ref[...]).astype(o_ref.dtype)


def _matmul(x, w, *, gain=None, out_dtype=f32, tm=512, tn=512, name="matmul"):
    M, K = x.shape
    N = w.shape[1]
    tm = min(tm, M)
    tn = max(t for t in range(LANES, min(tn, N) + 1, LANES) if N % t == 0)
    assert M % tm == 0 and N % LANES == 0, (M, N, tm, tn)
    grid = (M // tm, N // tn)
    x_spec = pl.BlockSpec((tm, K), lambda i, j: (i, 0))
    w_spec = pl.BlockSpec((K, tn), lambda i, j: (0, j))
    o_spec = pl.BlockSpec((tm, tn), lambda i, j: (i, j))
    out_shape = jax.ShapeDtypeStruct((M, N), out_dtype)
    if gain is None:
        return pl.pallas_call(_mm_kernel, grid=grid, in_specs=[x_spec, w_spec], out_specs=o_spec,
                              out_shape=out_shape, compiler_params=_params("parallel", "arbitrary"),
                              name=name)(x, w)
    return pl.pallas_call(
        _norm_mm_kernel, grid=grid,
        in_specs=[x_spec, pl.BlockSpec((1, K), lambda i, j: (0, 0)), w_spec],
        out_specs=o_spec, out_shape=out_shape, scratch_shapes=[pltpu.VMEM((tm, K), bf16)],
        compiler_params=_params("parallel", "arbitrary"), name=name)(x, gain.reshape(1, K), w)


_PROJ_SEGMENTS = (("nsa_q", 4, True), ("k_cmp", 1, True), ("v_cmp", 1, False), ("k_slc", 1, True),
                  ("v_slc", 1, False), ("k_win", 1, True), ("v_win", 1, False), ("moba_q", 4, True),
                  ("moba_k", 4, True), ("moba_v", 4, False), ("mem_q", 4, False), ("nsa_g", 1, False))
_PROJ_BLOCKS = sum(n for _, n, _ in _PROJ_SEGMENTS)


def _proj_kernel(x_ref, g_ref, w_ref, cos_ref, sin_ref, *o_refs, q_scale):
    xn = _rms(x_ref[...], g_ref[...]).astype(bf16)
    cos = cos_ref[...]
    sin = sin_ref[...]
    lane = lax.broadcasted_iota(jnp.int32, (1, LANES), 1)
    first_half = (lane & (HEAD_DIM // 2)) == 0

    def rope(a):
        partner = jnp.where(first_half, pltpu.roll(a, LANES - HEAD_DIM // 2, 1), pltpu.roll(a, HEAD_DIM // 2, 1))
        return a * cos + partner * sin

    c = 0
    for (name, n, rotary), o_ref in zip(_PROJ_SEGMENTS, o_refs):
        acc = _dot(xn, w_ref[:, c * LANES:(c + n) * LANES])
        c += n
        for r in range(n):
            a = acc[:, r * LANES:(r + 1) * LANES]
            if rotary:
                a = rope(a)
            if name == "nsa_q":
                a = a * q_scale
            if name == "nsa_g":
                a = jax.nn.sigmoid(a)
            o_ref[:, r * LANES:(r + 1) * LANES] = a.astype(o_ref.dtype)


def _proj_weight(w_in, head_order):
    D = w_in.shape[0]
    q = w_in[:, :NSA_Q_W].reshape(D, NSA_HEADS, HEAD_DIM)[:, np.array(head_order)].reshape(D, NSA_Q_W)
    gates = jnp.pad(w_in[:, NSA_Q_W:NSA_Q_W + NSA_GATE_W], ((0, 0), (0, LANES - NSA_GATE_W)))
    return jnp.concatenate([q, w_in[:, NSA_Q_W + NSA_GATE_W:], gates], axis=1).astype(bf16)


def _rope_tables(pos):
    half = HEAD_DIM // 2
    inv_freq = ROPE_THETA ** (-jnp.arange(half, dtype=f32) / half)
    ang = pos.astype(f32)[:, None] * inv_freq[None, :]
    cos, sin = jnp.cos(ang), jnp.sin(ang)
    reps = LANES // HEAD_DIM
    return jnp.tile(jnp.concatenate([cos, cos], axis=1), (1, reps)), jnp.tile(jnp.concatenate([-sin, sin], axis=1), (1, reps))


def _project(x, pos, lw, *, head_order, q_dtype, q_scale, tm=512):
    M, D = x.shape
    P = pos.shape[0]
    tm = min(tm, M, P)
    assert M % tm == 0 and P % tm == 0
    cos, sin = _rope_tables(pos)
    dtypes = dict(nsa_q=q_dtype, mem_q=bf16)
    out_shape = [jax.ShapeDtypeStruct((M, n * LANES), dtypes.get(name, f32)) for name, n, _ in _PROJ_SEGMENTS]
    out_specs = [pl.BlockSpec((tm, n * LANES), lambda i: (i, 0)) for _, n, _ in _PROJ_SEGMENTS]
    n_pos = P // tm
    outs = pl.pallas_call(
        functools.partial(_proj_kernel, q_scale=q_scale), grid=(M // tm,),
        in_specs=[pl.BlockSpec((tm, D), lambda i: (i, 0)), pl.BlockSpec((1, D), lambda i: (0, 0)),
                  pl.BlockSpec((D, _PROJ_BLOCKS * LANES), lambda i: (0, 0)),
                  pl.BlockSpec((tm, LANES), lambda i: (i % n_pos, 0)),
                  pl.BlockSpec((tm, LANES), lambda i: (i % n_pos, 0))],
        out_specs=out_specs, out_shape=out_shape, compiler_params=_params("parallel"),
        name="proj_in")(x, lw["norm_mix"].reshape(1, D), _proj_weight(lw["w_in"], head_order), cos, sin)
    return {name: o for (name, _, _), o in zip(_PROJ_SEGMENTS, outs)}


def _compress_kernel(rows_ref, nxt_ref, w1_ref, b1_ref, w2_ref, o_ref, buf_ref, *, n_chunks):
    stride = NSA_CMP_STRIDE
    n_rows = n_chunks * stride
    buf_ref[0:n_rows, :] = rows_ref[0]
    buf_ref[n_rows:n_rows + stride, :] = nxt_ref[0]
    acc = jnp.zeros((n_chunks, w1_ref.shape[-1]), f32)
    for p in range(0, NSA_CMP_LEN, 2):
        lhs = jnp.concatenate([buf_ref[pl.ds(p, n_chunks, stride=stride), :],
                               buf_ref[pl.ds(p + 1, n_chunks, stride=stride), :]], axis=1)
        acc = acc + _dot(lhs.astype(bf16), w1_ref[p // 2])
    hid = jax.nn.gelu(acc + b1_ref[...])
    o_ref[0] = _dot(hid.astype(bf16), w2_ref[...])


def _compress(rows, w1, b1, w2, *, chunks_per_step):
    B, L, W = rows.shape
    G, dh, stride = NSA_GROUPS, HEAD_DIM, NSA_CMP_STRIDE
    n_total = L // stride
    C = min(chunks_per_step, n_total)
    assert W == G * dh and L % stride == 0 and n_total % C == 0 and NSA_CMP_LEN == 2 * stride
    w1bd, b1bd, w2bd = _compress_weights(w1, b1, w2)
    n_steps = n_total // C
    last_blk = L // stride - 1
    return pl.pallas_call(
        functools.partial(_compress_kernel, n_chunks=C), grid=(B, n_steps),
        in_specs=[pl.BlockSpec((1, C * stride, W), lambda b, i: (b, i, 0)),
                  pl.BlockSpec((1, stride, W), lambda b, i: (b, jnp.minimum((i + 1) * C, last_blk), 0)),
                  pl.BlockSpec(w1bd.shape, lambda b, i: (0, 0, 0)),
                  pl.BlockSpec(b1bd.shape, lambda b, i: (0, 0)),
                  pl.BlockSpec(w2bd.shape, lambda b, i: (0, 0))],
        out_specs=pl.BlockSpec((1, C, W), lambda b, i: (b, i, 0)),
        out_shape=jax.ShapeDtypeStruct((B, n_total, W), f32),
        scratch_shapes=[pltpu.VMEM((C * stride + stride, W), f32)],
        compiler_params=_params("parallel", "arbitrary"), name="compress")(rows, rows, w1bd, b1bd, w2bd)


def _flash_tile(q, kt, vt, mask, m, l, acc, rep):
    tq, tk = mask.shape
    s = _dot_nt(q, kt.astype(bf16)).reshape(rep, tq, tk)
    s = jnp.where(mask[None], s, MASKED)
    m_new = jnp.maximum(m, jnp.max(s, axis=-1, keepdims=True))
    p = jnp.exp(s - m_new)
    alpha = jnp.exp(m - m_new)
    l = alpha * l + jnp.sum(p, axis=-1, keepdims=True)
    pv = _dot(p.reshape(rep * tq, tk).astype(bf16), vt.astype(bf16))
    acc = alpha.reshape(rep * tq, 1) * acc + pv
    return m_new, l, acc


def _softmax_tile(q, kt, vt, mask, rep):
    tq, tk = mask.shape
    s = jnp.where(mask[None], _dot_nt(q, kt).reshape(rep, tq, tk), MASKED)
    m = jnp.maximum(jnp.max(s, axis=-1, keepdims=True), ROW_MAX_INIT)
    e = jnp.exp(s - m)
    den = jnp.sum(e, axis=-1, keepdims=True)
    p = e / jnp.where(den > 0, den, 1.0)
    return p, _dot(p.reshape(rep * tq, tk).astype(bf16), vt)


def _flash_init(rep, tq):
    return (jnp.full((rep, tq, 1), ROW_MAX_INIT, f32), jnp.zeros((rep, tq, 1), f32),
            jnp.zeros((rep * tq, LANES), f32))


def _flash_finish(l, acc):
    l = l.reshape(acc.shape[0], 1)
    return acc / jnp.where(l > 0, l, 1.0)


def _topk_mask(score, k, col_index, n_real):
    rank = jnp.zeros(score.shape, jnp.int32)
    for i in range(n_real):
        si = score[:, i:i + 1]
        ahead = (si > score) | ((si == score) & (col_index > i))
        rank = rank + jnp.where(ahead, 1, 0)
    return rank < k


def _mask_to_bf16(mask):
    return jnp.where(mask, 1.0, 0.0).astype(bf16)


def _nsa_prompt_kernel(q_ref, g_ref, ck_ref, cv_ref, k_ref, v_ref, wk_ref, wv_ref,
                       imp_ref, esel_ref, o_ref, selexp_ref, *, tq, tk, n_sel):
    rep = NSA_REP
    s0 = pl.program_id(1) * tq
    qpos = s0 + lax.broadcasted_iota(jnp.int32, (tq, 1), 0)
    n_cmp = ck_ref.shape[1]
    cmp_end = lax.broadcasted_iota(jnp.int32, (1, n_cmp), 1) * NSA_CMP_STRIDE + (NSA_CMP_LEN - 1)
    cmp_mask = cmp_end <= qpos
    lane = lax.broadcasted_iota(jnp.int32, (1, LANES), 1)
    cur = jnp.right_shift(qpos, NSA_SEL_BLOCK.bit_length() - 1)
    visible = lane <= cur
    forced = (lane == 0) | (lane == cur) | (lane == cur - 1)
    ck = ck_ref[0].astype(bf16)
    cv = cv_ref[0].astype(bf16)
    groups = range(NSA_GROUPS)

    def group_query(g):
        in_group = (lane >= g * HEAD_DIM) & (lane < (g + 1) * HEAD_DIM)
        return jnp.concatenate(
            [jnp.where(in_group, q_ref[0, :, r * LANES:(r + 1) * LANES], jnp.zeros((), bf16)) for r in range(rep)],
            axis=0)

    q = [group_query(g) for g in groups]

    o_c = []
    for g in groups:
        p_c, o = _softmax_tile(q[g], ck, cv, cmp_mask, rep)
        o_c.append(o)
        imp = _dot(jnp.sum(p_c, axis=0).astype(bf16), imp_ref[...])
        score = jnp.where(visible, jnp.where(forced, FORCED, imp), -FORCED)
        sel = _topk_mask(score, NSA_TOPN, lane, n_sel) & visible
        selexp_ref[g] = _dot(_mask_to_bf16(sel), esel_ref[...])

    def slc_body(t, carry):
        k0 = pl.multiple_of(t * tk, tk)
        causal = (k0 + lax.broadcasted_iota(jnp.int32, (1, tk), 1)) <= qpos
        kt = k_ref[0, pl.ds(k0, tk), :].astype(bf16)
        vt = v_ref[0, pl.ds(k0, tk), :].astype(bf16)
        return tuple(_flash_tile(q[g], kt, vt, (selexp_ref[g, :, pl.ds(k0, tk)] > 0.5) & causal, *carry[g], rep)
                     for g in groups)

    slc = lax.fori_loop(0, (s0 + tq + tk - 1) // tk, slc_body, tuple(_flash_init(rep, tq) for g in groups))
    o_s = [_flash_finish(l, acc) for _, l, acc in slc]

    n_win = NSA_WINDOW + tq
    w0 = pl.multiple_of(jnp.maximum(s0 - NSA_WINDOW, 0), tq)
    kpos = w0 + lax.broadcasted_iota(jnp.int32, (1, n_win), 1)
    wmask = (kpos <= qpos) & (kpos > qpos - NSA_WINDOW)
    wk = wk_ref[0, pl.ds(w0, n_win), :].astype(bf16)
    wv = wv_ref[0, pl.ds(w0, n_win), :].astype(bf16)
    o_w = [_softmax_tile(q[g], wk, wv, wmask, rep)[1] for g in groups]

    for r in range(rep):
        halves = []
        for g in groups:
            c = 3 * (g * rep + r)
            rows = slice(r * tq, (r + 1) * tq)
            halves.append(g_ref[0, :, c:c + 1] * o_c[g][rows] + g_ref[0, :, c + 1:c + 2] * o_s[g][rows]
                          + g_ref[0, :, c + 2:c + 3] * o_w[g][rows])
        o_ref[0, :, r * LANES:(r + 1) * LANES] = jnp.where(lane < HEAD_DIM, halves[0], halves[1]).astype(o_ref.dtype)


def _nsa_static_maps(n_cmp_pad, n_sel):
    ratio_c = NSA_CMP_LEN // NSA_CMP_STRIDE
    ratio_s = NSA_SEL_BLOCK // NSA_CMP_STRIDE
    nc = n_cmp_pad - ratio_c + 1
    mat = np.zeros((n_cmp_pad, n_sel), np.float32)
    j = np.arange(n_sel)
    for mm in range(ratio_s):
        for n in range(ratio_c):
            i = ratio_s * j + mm - n
            ok = (i >= 0) & (i < nc)
            np.add.at(mat, (i[ok], j[ok]), 1.0)
    return mat


def _block_expand(n_blocks, block, total):
    k = np.arange(total)
    return (k[None, :] // block == np.arange(n_blocks)[:, None]).astype(np.float32)


def _nsa_prompt(q, gates, c_k, c_v, k_slc, v_slc, k_win, v_win, *, tq=128, tk=256):
    B, S, _ = q.shape
    n_cmp = S // NSA_CMP_STRIDE
    n_sel = max(S // NSA_SEL_BLOCK, NSA_TOPN)
    assert S % tq == 0 and S % tk == 0 and NSA_WINDOW % tq == 0 and S >= NSA_WINDOW + tq
    assert n_sel * NSA_SEL_BLOCK == S and n_sel <= LANES and c_k.shape[1] == n_cmp
    imp = jnp.asarray(np.pad(_nsa_static_maps(n_cmp, n_sel), ((0, 0), (0, LANES - n_sel))), bf16)
    esel = jnp.asarray(_block_expand(LANES, NSA_SEL_BLOCK, S), bf16)
    per_b = lambda rows: pl.BlockSpec((1, rows, LANES), lambda b, i: (b, 0, 0))
    full = lambda shape: pl.BlockSpec(shape, lambda b, i: (0,) * len(shape))
    return pl.pallas_call(
        functools.partial(_nsa_prompt_kernel, tq=tq, tk=tk, n_sel=n_sel),
        grid=(B, S // tq),
        in_specs=[pl.BlockSpec((1, tq, NSA_Q_W), lambda b, i: (b, i, 0)),
                  pl.BlockSpec((1, tq, LANES), lambda b, i: (b, i, 0)),
                  per_b(n_cmp), per_b(n_cmp), per_b(S), per_b(S), per_b(S), per_b(S),
                  full((n_cmp, LANES)), full((LANES, S))],
        out_specs=pl.BlockSpec((1, tq, NSA_Q_W), lambda b, i: (b, i, 0)),
        out_shape=jax.ShapeDtypeStruct((B, S, NSA_Q_W), bf16),
        scratch_shapes=[pltpu.VMEM((NSA_GROUPS, tq, S), f32)],
        compiler_params=_params("parallel", "arbitrary"),
        name="nsa_prompt")(q, gates, c_k, c_v, k_slc, v_slc, k_win, v_win, imp, esel)


def _moba_prompt_kernel(q_ref, k_ref, v_ref, ablk_ref, o_ref, means_ref, *, tq, n_blk):
    tk = MOBA_BLOCK
    n_pairs = q_ref.shape[-1] // LANES
    qi = pl.program_id(1)
    s0 = qi * tq
    cur = s0 // MOBA_BLOCK
    qpos = s0 + (lax.broadcasted_iota(jnp.int32, (2 * tq, 1), 0) & (tq - 1))
    lane = lax.broadcasted_iota(jnp.int32, (1, LANES), 1)
    low = lane < HEAD_DIM
    past = lane < cur

    @pl.when(qi == 0)
    def _():
        for j in range(n_pairs):
            kj = k_ref[0, :, j * LANES:(j + 1) * LANES]
            k_hi, k_lo = _split_hi_lo(kj)
            k_lo2 = (kj - k_hi.astype(f32) - k_lo.astype(f32)).astype(bf16)
            means_ref[j] = (_dot(ablk_ref[...], k_hi) + _dot(ablk_ref[...], k_lo)
                            + _dot(ablk_ref[...], k_lo2))

    pairs = range(n_pairs)
    cols = [slice(j * LANES, (j + 1) * LANES) for j in pairs]
    q, sel = [], []
    for j in pairs:
        qj = q_ref[0, :, cols[j]]
        qf = jnp.concatenate([jnp.where(low, qj, 0.0), jnp.where(low, 0.0, qj)], axis=0)
        q.append((qf * (HEAD_DIM ** -0.5)).astype(bf16))
        gate = _dot_nt(q[j], means_ref[j].astype(bf16))
        score = jnp.where(past, gate, -FORCED)
        picked = (_topk_mask(score, MOBA_TOPK, lane, n_blk) & past) | (lane == cur)
        sel.append(jnp.where(picked, 1.0, 0.0))

    def body(t, carry):
        k0 = pl.multiple_of(t * tk, tk)
        causal = (k0 + lax.broadcasted_iota(jnp.int32, (1, tk), 1)) <= qpos
        out = []
        for j in pairs:
            block_on = jnp.max(jnp.where(lane == t, sel[j], 0.0), axis=1, keepdims=True) > 0.5
            out.append(_flash_tile(q[j], k_ref[0, pl.ds(k0, tk), cols[j]], v_ref[0, pl.ds(k0, tk), cols[j]],
                                   block_on & causal, *carry[j], 1))
        return tuple(out)

    state = lax.fori_loop(0, cur + 1, body, tuple(_flash_init(1, 2 * tq) for j in pairs))
    for j in pairs:
        o = _flash_finish(state[j][1], state[j][2])
        o_ref[0, :, cols[j]] = jnp.where(low, o[:tq], o[tq:]).astype(o_ref.dtype)


def _moba_prompt(q, k, v, *, tq=256):
    B, S, W = q.shape
    n_blk = S // MOBA_BLOCK
    assert S % MOBA_BLOCK == 0 and MOBA_BLOCK % tq == 0 and tq & (tq - 1) == 0
    assert MOBA_TOPK <= n_blk <= LANES and W % LANES == 0
    ablk = jnp.asarray(_block_expand(LANES, MOBA_BLOCK, S) / MOBA_BLOCK, bf16)
    per_b = pl.BlockSpec((1, S, W), lambda b, i: (b, 0, 0))
    return pl.pallas_call(
        functools.partial(_moba_prompt_kernel, tq=tq, n_blk=n_blk),
        grid=(B, S // tq),
        in_specs=[pl.BlockSpec((1, tq, W), lambda b, i: (b, i, 0)), per_b, per_b,
                  pl.BlockSpec((LANES, S), lambda b, i: (0, 0))],
        out_specs=pl.BlockSpec((1, tq, W), lambda b, i: (b, i, 0)),
        out_shape=jax.ShapeDtypeStruct((B, S, W), bf16),
        scratch_shapes=[pltpu.VMEM((W // LANES, LANES, LANES), f32)],
        compiler_params=_params("parallel", "arbitrary"),
        name="moba_prompt")(q, k, v, ablk)


def _mem_attn_kernel(q_ref, kv_ref, o_ref):
    d = MEM_HEAD_DIM
    for h in range(MEM_HEADS):
        k = kv_ref[0, :, h * d:(h + 1) * d].astype(bf16)
        v = kv_ref[0, :, MEM_W + h * d:MEM_W + (h + 1) * d].astype(bf16)
        s = _dot_nt(q_ref[0, :, h * d:(h + 1) * d], k) * (d ** -0.5)
        e = jnp.exp(s - jnp.max(s, axis=-1, keepdims=True))
        p = e / jnp.sum(e, axis=-1, keepdims=True)
        o_ref[0, :, h * d:(h + 1) * d] = _dot(p.astype(bf16), v).astype(o_ref.dtype)


def _mem_attn(q, kv, *, tq=512):
    B, T, W = q.shape
    M = kv.shape[1]
    tq = min(tq, T)
    assert T % tq == 0
    return pl.pallas_call(
        _mem_attn_kernel, grid=(B, T // tq),
        in_specs=[pl.BlockSpec((1, tq, W), lambda b, i: (b, i, 0)),
                  pl.BlockSpec((1, M, 2 * W), lambda b, i: (b, 0, 0))],
        out_specs=pl.BlockSpec((1, tq, W), lambda b, i: (b, i, 0)),
        out_shape=jax.ShapeDtypeStruct((B, T, W), bf16),
        compiler_params=_params("parallel", "arbitrary"), name="mem_attn")(q, kv)


def _merge_kernel(x_ref, gmix_ref, on_ref, om_ref, oe_ref, wg_ref, bg_ref, wn_ref, wm_ref, we_ref, wo_ref,
                  gffn_ref, wr_ref, br_ref, xmid_ref, h_ref, logit_ref):
    D = x_ref.shape[1]
    xf = x_ref[...]
    hn = _rms(xf, gmix_ref[...]).astype(bf16)
    mixed = None
    for i, (o_ref, w_ref) in enumerate(((on_ref, wn_ref), (om_ref, wm_ref), (oe_ref, we_ref))):
        gate = jax.nn.sigmoid(_dot(hn, wg_ref[:, i * D:(i + 1) * D]) + bg_ref[:, i * D:(i + 1) * D])
        term = gate * _dot(o_ref[...], w_ref[...])
        mixed = term if mixed is None else mixed + term
    xm = xf + _dot(mixed.astype(bf16), wo_ref[...])
    xmid_ref[...] = xm
    y = _rms(xm, gffn_ref[...])
    h_ref[...] = y.astype(bf16)
    logit_ref[...] = _dot(y.astype(bf16), wr_ref[...]) + br_ref[...]


def _merge(x, o_nsa, o_moba, o_mem, lw, *, nsa_head_order, tm=512):
    M, D = x.shape
    tm = min(tm, M)
    assert M % tm == 0
    n_route = N_GROUPS + N_EXPERTS
    w_route = jnp.pad(jnp.concatenate([lw["w_group"], lw["w_router"]], axis=1), ((0, 0), (0, LANES - n_route)))
    b_route = jnp.pad(jnp.concatenate([lw["b_group"], lw["b_router"]]), (0, LANES - n_route)).reshape(1, LANES)
    w_nsa = lw["w_nsa_out"].reshape(NSA_HEADS, HEAD_DIM, D)[np.array(nsa_head_order)].reshape(NSA_Q_W, D)
    row = lambda w: pl.BlockSpec((tm, w), lambda i: (i, 0))
    const = lambda a: pl.BlockSpec(a.shape, lambda i: (0,) * a.ndim)
    consts = [lw["w_merge_gate"].astype(bf16), lw["b_merge_gate"].reshape(1, -1), w_nsa.astype(bf16),
              lw["w_moba_out"].astype(bf16), lw["w_mem_out"].astype(bf16), lw["w_out"].astype(bf16),
              lw["norm_ffn"].reshape(1, D), w_route.astype(bf16), b_route]
    gmix = lw["norm_mix"].reshape(1, D)
    return pl.pallas_call(
        _merge_kernel, grid=(M // tm,),
        in_specs=[row(D), const(gmix), row(NSA_Q_W), row(MOBA_W), row(MEM_W)] + [const(a) for a in consts],
        out_specs=[row(D), row(D), row(LANES)],
        out_shape=[jax.ShapeDtypeStruct((M, D), f32), jax.ShapeDtypeStruct((M, D), bf16),
                   jax.ShapeDtypeStruct((M, LANES), f32)],
        compiler_params=_params("parallel"), name="merge")(x, gmix, o_nsa, o_moba, o_mem, *consts)


def _expert_kernel(be_ref, x_ref, wg_ref, wu_ref, wd_ref, o_ref, wg_s, wu_s, wd_s):
    i = pl.program_id(0)

    @pl.when((i == 0) | (be_ref[i] != be_ref[jnp.maximum(i - 1, 0)]))
    def _():
        wg_s[...] = wg_ref[0].astype(bf16)
        wu_s[...] = wu_ref[0].astype(bf16)
        wd_s[...] = wd_ref[0].astype(bf16)

    x = x_ref[...]
    a = _dot(x, wg_s[...])
    u = _dot(x, wu_s[...])
    mid = (a * jax.nn.sigmoid(a)) * u
    o_ref[...] = _dot(mid.astype(bf16), wd_s[...])


def _expert_ffn(blk_expert, xs, w_gate, w_up, w_down):
    cap, D = xs.shape
    F = w_gate.shape[-1]
    return pl.pallas_call(
        _expert_kernel,
        grid_spec=pltpu.PrefetchScalarGridSpec(
            num_scalar_prefetch=1, grid=(cap // MOE_ROWS,),
            in_specs=[pl.BlockSpec((MOE_ROWS, D), lambda i, be: (i, 0)),
                      pl.BlockSpec((1, D, F), lambda i, be: (be[i], 0, 0)),
                      pl.BlockSpec((1, D, F), lambda i, be: (be[i], 0, 0)),
                      pl.BlockSpec((1, F, D), lambda i, be: (be[i], 0, 0))],
            out_specs=pl.BlockSpec((MOE_ROWS, D), lambda i, be: (i, 0)),
            scratch_shapes=[pltpu.VMEM((D, F), bf16), pltpu.VMEM((D, F), bf16), pltpu.VMEM((F, D), bf16)]),
        out_shape=jax.ShapeDtypeStruct((cap, D), f32),
        compiler_params=_params("arbitrary"), name="expert_ffn")(blk_expert, xs, w_gate, w_up, w_down)


def _moe_rows(h_parts, logits, lw):
    n_tok, D = logits.shape[0], h_parts[0].shape[1]
    n_route = N_GROUPS + N_EXPERTS
    g_logits = logits[:, :N_GROUPS]
    g_prob = jax.nn.softmax(g_logits, axis=-1)
    grp = jnp.argmax(g_logits, axis=-1).astype(jnp.int32)
    e_logits = logits[:, N_GROUPS:n_route].reshape(-1, N_GROUPS, EXPERTS_PER_GROUP)
    e_in = jnp.take_along_axis(e_logits, grp[:, None, None], axis=1)[:, 0]
    top_p, top_i = lax.top_k(jax.nn.softmax(e_in, axis=-1), TOP_K_IN_GROUP)
    weight = top_p / jnp.sum(top_p, axis=-1, keepdims=True) * jnp.take_along_axis(g_prob, grp[:, None], axis=1)
    expert = grp[:, None] * EXPERTS_PER_GROUP + top_i.astype(jnp.int32)
    n_assign = n_tok * TOP_K_IN_GROUP
    n_blocks = -(-n_assign // MOE_ROWS) + N_EXPERTS
    cap = n_blocks * MOE_ROWS
    flat_e = expert.reshape(-1)
    order = jnp.argsort(flat_e, stable=True).astype(jnp.int32)
    rank = jnp.argsort(order).astype(jnp.int32)
    sort_end = jnp.searchsorted(flat_e[order], jnp.arange(N_EXPERTS, dtype=jnp.int32), side="right").astype(jnp.int32)
    counts = sort_end - jnp.concatenate([jnp.zeros((1,), jnp.int32), sort_end[:-1]])
    sort_start = sort_end - counts
    padded = (counts + MOE_ROWS - 1) // MOE_ROWS * MOE_ROWS
    pad_end = jnp.cumsum(padded)
    pad_start = pad_end - padded
    blk_expert = jnp.minimum(
        jnp.searchsorted(pad_end, jnp.arange(n_blocks, dtype=jnp.int32) * MOE_ROWS, side="right"),
        N_EXPERTS - 1).astype(jnp.int32)
    slot_e = jnp.repeat(blk_expert, MOE_ROWS)
    slot_r = jnp.arange(cap, dtype=jnp.int32) - pad_start[slot_e]
    slot_pair = order[jnp.clip(sort_start[slot_e] + slot_r, 0, n_assign - 1)]
    slot_tok = jnp.where(slot_r < counts[slot_e], slot_pair // TOP_K_IN_GROUP, n_tok)
    dest = pad_start[flat_e] + rank - sort_start[flat_e]
    h_pad = jnp.concatenate(list(h_parts) + [jnp.zeros((1, D), h_parts[0].dtype)], axis=0)
    ys = _expert_ffn(blk_expert, h_pad[slot_tok], lw["w_expert_gate"], lw["w_expert_up"], lw["w_expert_down"])
    return ys[dest].reshape(n_tok, TOP_K_IN_GROUP * D), weight


def _combine_kernel(x_ref, p_ref, w_ref, g_ref, o_ref, *, normalise):
    D = x_ref.shape[1]
    y = x_ref[...] + (w_ref[:, 0:1] * p_ref[:, :D] + w_ref[:, 1:2] * p_ref[:, D:])
    o_ref[...] = _rms(y, g_ref[...]) if normalise else y


def _combine(x, picked, weight, gain, *, normalise, tm=512):
    M, D = x.shape
    tm = min(tm, M)
    assert M % tm == 0 and TOP_K_IN_GROUP == 2
    return pl.pallas_call(
        functools.partial(_combine_kernel, normalise=normalise), grid=(M // tm,),
        in_specs=[pl.BlockSpec((tm, D), lambda i: (i, 0)), pl.BlockSpec((tm, 2 * D), lambda i: (i, 0)),
                  pl.BlockSpec((tm, 2), lambda i: (i, 0)), pl.BlockSpec((1, D), lambda i: (0, 0))],
        out_specs=pl.BlockSpec((tm, D), lambda i: (i, 0)),
        out_shape=jax.ShapeDtypeStruct((M, D), f32),
        compiler_params=_params("parallel"), name="combine")(x, picked, weight, gain.reshape(1, D))


def _cache_pages(cache):
    return jnp.transpose(cache, (0, 2, 3, 1))


def _page_rows(buf_ref, slot, page, heads=None):
    h0, h1 = heads if heads is not None else (0, buf_ref.shape[2])
    slab = buf_ref[slot, page, h0:h1]
    return slab.reshape((h1 - h0) * slab.shape[1], slab.shape[2]).T


class _PageStream:
    def __init__(self, pt_ref, pool_refs, buf_refs, sem_ref, pages_per_chunk):
        self.pt_ref, self.pools, self.bufs, self.sem = pt_ref, pool_refs, buf_refs, sem_ref
        self.P = pages_per_chunk
        self.n_chunks = pl.num_programs(1)
        self.b, self.c = pl.program_id(0), pl.program_id(1)
        self.step = self.b * self.n_chunks + self.c
        self.slot = self.step % 2

    def _copies(self, b, c, slot):
        out = []
        for k, (pool, buf) in enumerate(zip(self.pools, self.bufs)):
            for p in range(self.P):
                out.append(pltpu.make_async_copy(pool.at[self.pt_ref[b, c * self.P + p]], buf.at[slot, p],
                                                 self.sem.at[k, slot]))
        return out

    def extra_copies(self, b, c, slot):
        return []

    def _start(self, b, c, slot):
        for cp in self._copies(b, c, slot) + self.extra_copies(b, c, slot):
            cp.start()

    def advance(self):
        @pl.when(self.step == 0)
        def _():
            self._start(self.b, self.c, self.slot)

        last_c = self.c == self.n_chunks - 1
        nb = jnp.where(last_c, self.b + 1, self.b)
        nc = jnp.where(last_c, 0, self.c + 1)

        @pl.when(self.step + 1 < pl.num_programs(0) * self.n_chunks)
        def _():
            self._start(nb, nc, 1 - self.slot)

        for cp in self._copies(self.b, self.c, self.slot) + self.extra_copies(self.b, self.c, self.slot):
            cp.wait()
        return self.slot


class _CompressStream(_PageStream):
    def extra_copies(self, b, c, slot):
        n_pages = self.pt_ref.shape[1]
        nxt = jnp.minimum((c + 1) * self.P, n_pages - 1)
        return [pltpu.make_async_copy(self.pools[0].at[self.pt_ref[b, nxt]], self.bufs[0].at[slot, self.P],
                                      self.sem.at[1, slot])]


def _compress_paged_kernel(pt_ref, pool_ref, w1_ref, b1_ref, w2_ref, o_ref, buf_ref, sem_ref, rows_ref, *, pages):
    stride = NSA_CMP_STRIDE
    n_chunks = pages * PAGE_SIZE // stride
    slot = _CompressStream(pt_ref, [pool_ref], [buf_ref], sem_ref, pages).advance()
    for p in range(pages + 1):
        rows_ref[p * PAGE_SIZE:(p + 1) * PAGE_SIZE, :] = _page_rows(buf_ref, slot, p)
    acc = jnp.zeros((n_chunks, w1_ref.shape[-1]), f32)
    for p in range(0, NSA_CMP_LEN, 2):
        lhs = jnp.concatenate([rows_ref[pl.ds(p, n_chunks, stride=stride), :],
                               rows_ref[pl.ds(p + 1, n_chunks, stride=stride), :]], axis=1)
        acc = acc + _dot(lhs.astype(bf16), w1_ref[p // 2])
    hid = jax.nn.gelu(acc + b1_ref[...])
    o_ref[0] = _dot(hid.astype(bf16), w2_ref[...])


def _compress_weights(w1, b1, w2):
    G, dh = NSA_GROUPS, HEAD_DIM
    hidden = w1.shape[1]
    eye = jnp.eye(G, dtype=f32)
    w1p = w1.reshape(NSA_CMP_LEN, dh, hidden)
    w1bd = jnp.einsum("gh,pdn->pgdhn", eye, w1p).reshape(NSA_CMP_LEN // 2, 2 * G * dh, G * hidden).astype(bf16)
    w2bd = jnp.einsum("gh,nd->gnhd", eye, w2).reshape(G * hidden, G * dh).astype(bf16)
    return w1bd, jnp.tile(b1, G).reshape(1, G * hidden), w2bd


def _compress_paged(pool, page_table, w1, b1, w2, *, pages=32):
    B, n_pages = page_table.shape
    W = pool.shape[1] * pool.shape[2]
    assert pool.shape[3] == PAGE_SIZE and W == LANES
    assert n_pages % pages == 0 and PAGE_SIZE % NSA_CMP_STRIDE == 0 and NSA_CMP_LEN == 2 * NSA_CMP_STRIDE
    w1bd, b1bd, w2bd = _compress_weights(w1, b1, w2)
    C = pages * PAGE_SIZE // NSA_CMP_STRIDE
    n_total = n_pages * PAGE_SIZE // NSA_CMP_STRIDE
    const = lambda a: pl.BlockSpec(a.shape, lambda b, c, pt: (0,) * a.ndim)
    return pl.pallas_call(
        functools.partial(_compress_paged_kernel, pages=pages),
        grid_spec=pltpu.PrefetchScalarGridSpec(
            num_scalar_prefetch=1, grid=(B, n_pages // pages),
            in_specs=[pl.BlockSpec(memory_space=pl.ANY), const(w1bd), const(b1bd), const(w2bd)],
            out_specs=pl.BlockSpec((1, C, W), lambda b, c, pt: (b, c, 0)),
            scratch_shapes=[pltpu.VMEM((2, pages + 1) + pool.shape[1:], f32), pltpu.SemaphoreType.DMA((2, 2)),
                            pltpu.VMEM(((pages + 1) * PAGE_SIZE, W), f32)]),
        out_shape=jax.ShapeDtypeStruct((B, n_total, W), f32),
        compiler_params=_params("arbitrary", "arbitrary"), name="compress_paged")(page_table, pool, w1bd, b1bd, w2bd)


def _rank_rows(score, k, n_real):
    row = lax.broadcasted_iota(jnp.int32, (score.shape[0], 1), 0)
    rank = jnp.zeros(score.shape, jnp.int32)
    for i in range(n_real):
        si = score[i:i + 1, :]
        ahead = (si > score) | ((si == score) & (row > i))
        rank = rank + jnp.where(ahead, 1, 0)
    return rank < k


def _expand_lanes(a, e_ref):
    hi, lo = _split_hi_lo(a)
    return _dot(hi, e_ref[...]) + _dot(lo, e_ref[...])


def _merge_partials(sel, m, l, acc, e_ref):
    mx = jnp.max(jnp.where(sel, m, MASKED), axis=0, keepdims=True)
    w = jnp.where(sel, jnp.exp(m - mx), 0.0)
    den = jnp.sum(w * l, axis=0, keepdims=True)
    num = jnp.sum(_expand_lanes(w, e_ref) * acc, axis=0, keepdims=True)
    den_w = _expand_lanes(jnp.broadcast_to(den, (HEAD_ROWS, LANES)), e_ref)[0:1]
    return num / den_w


def _block_partial(s, v, e_ref):
    m = jnp.max(s, axis=0, keepdims=True)
    p = jnp.exp(s - m)
    l = jnp.sum(p, axis=0, keepdims=True)
    acc = jnp.sum(_dot(p.astype(bf16), e_ref[...]) * v, axis=0, keepdims=True)
    return m, l, acc


def _moba_decode_kernel(pt_ref, qs_ref, new_ref, kpool_ref, vpool_ref, e_ref, o_ref,
                        kbuf_ref, vbuf_ref, sem_ref, m_ref, l_ref, acc_ref, means_ref, *, pages, n_blk):
    blocks_per_chunk = pages * PAGE_SIZE // MOBA_BLOCK
    stream = _PageStream(pt_ref, [kpool_ref, vpool_ref], [kbuf_ref, vbuf_ref], sem_ref, pages)
    slot = stream.advance()
    c = stream.c
    qs = qs_ref[0]
    pages_per_block = MOBA_BLOCK // PAGE_SIZE
    pairs = kbuf_ref.shape[2] * kbuf_ref.shape[3] // LANES

    def block_rows(buf_ref, i):
        return jnp.concatenate(
            [jnp.concatenate([_page_rows(buf_ref, slot, i * pages_per_block + p, (2 * j, 2 * j + 2))
                              for j in range(pairs)], axis=1) for p in range(pages_per_block)], axis=0)

    for i in range(blocks_per_chunk):
        kb = block_rows(kbuf_ref, i)
        j = c * blocks_per_chunk + i
        means_ref[pl.ds(j, 1), :] = jnp.sum(kb, axis=0, keepdims=True) * (1.0 / MOBA_BLOCK)
        m, l, acc = _block_partial(_dot_nt(kb.astype(bf16), qs), block_rows(vbuf_ref, i), e_ref)
        m_ref[pl.ds(j, 1), :] = m
        l_ref[pl.ds(j, 1), :] = l
        acc_ref[pl.ds(j, 1), :] = acc

    @pl.when(c == stream.n_chunks - 1)
    def _():
        n_rows, W = acc_ref.shape
        tail = n_rows - n_blk
        k_new = jnp.broadcast_to(new_ref[0, 0:1, :], (HEAD_ROWS, W)).astype(bf16)
        s_new = _dot_nt(k_new, qs)[0:1]
        m_ref[pl.ds(n_blk, tail), :] = jnp.broadcast_to(s_new, (tail, LANES))
        l_ref[pl.ds(n_blk, tail), :] = jnp.ones((tail, LANES), f32)
        acc_ref[pl.ds(n_blk, tail), :] = jnp.broadcast_to(new_ref[0, 1:2, :], (tail, W))
        means_ref[pl.ds(n_blk, tail), :] = jnp.zeros((tail, W), f32)
        gate = _dot_nt(means_ref[...].astype(bf16), qs)
        row = lax.broadcasted_iota(jnp.int32, (n_rows, 1), 0)
        sel = (_rank_rows(gate, MOBA_TOPK, n_blk) & (row < n_blk)) | (row == n_blk)
        o_ref[0] = jnp.broadcast_to(_merge_partials(sel, m_ref[...], l_ref[...], acc_ref[...], e_ref), o_ref.shape[1:])


def _head_rows(q, n_heads, dtype):
    B = q.shape[0]
    dh = q.shape[1] // n_heads
    eye = jnp.eye(n_heads, dtype=q.dtype)
    rows = jnp.einsum("bhd,hg->bhgd", q.reshape(B, n_heads, dh), eye).reshape(B, n_heads, n_heads * dh)
    return jnp.pad(rows, ((0, 0), (0, LANES - n_heads), (0, 0))).astype(dtype)


def _head_expand(n_heads, dh):
    e = np.zeros((LANES, n_heads * dh), np.float32)
    for h in range(n_heads):
        e[h, h * dh:(h + 1) * dh] = 1.0
    return e


def _moba_decode(q, k_new, v_new, k_pool, v_pool, page_table, *, pages=8):
    B, W = q.shape
    n_pages = page_table.shape[1]
    past = n_pages * PAGE_SIZE
    n_blk = past // MOBA_BLOCK
    assert past % MOBA_BLOCK == 0 and n_pages % pages == 0 and (pages * PAGE_SIZE) % MOBA_BLOCK == 0
    assert MOBA_TOPK <= n_blk and k_pool.shape[1:] == (MOBA_HEADS, HEAD_DIM, PAGE_SIZE)
    qs = _head_rows(q * (HEAD_DIM ** -0.5), MOBA_HEADS, bf16)
    new = jnp.pad(jnp.stack([k_new, v_new], axis=1), ((0, 0), (0, 6), (0, 0)))
    e = jnp.asarray(_head_expand(MOBA_HEADS, HEAD_DIM), bf16)
    n_rows = _round_up(n_blk + 1, 8)
    per_b = lambda rows: pl.BlockSpec((1, rows, W), lambda b, c, pt: (b, 0, 0))
    out = pl.pallas_call(
        functools.partial(_moba_decode_kernel, pages=pages, n_blk=n_blk),
        grid_spec=pltpu.PrefetchScalarGridSpec(
            num_scalar_prefetch=1, grid=(B, n_pages // pages),
            in_specs=[per_b(LANES), per_b(8), pl.BlockSpec(memory_space=pl.ANY),
                      pl.BlockSpec(memory_space=pl.ANY), pl.BlockSpec(e.shape, lambda b, c, pt: (0, 0))],
            out_specs=per_b(8),
            scratch_shapes=[pltpu.VMEM((2, pages) + k_pool.shape[1:], f32), pltpu.VMEM((2, pages) + v_pool.shape[1:], f32),
                            pltpu.SemaphoreType.DMA((2, 2)),
                            pltpu.VMEM((n_rows, LANES), f32), pltpu.VMEM((n_rows, LANES), f32),
                            pltpu.VMEM((n_rows, W), f32), pltpu.VMEM((n_rows, W), f32)]),
        out_shape=jax.ShapeDtypeStruct((B, 8, W), f32),
        compiler_params=_params("arbitrary", "arbitrary"), name="moba_decode")(
            page_table, qs, new, k_pool, v_pool, e)
    return out[:, 0]


def _nsa_decode_kernel(pt_ref, qn_ref, g_ref, new_ref, ck_ref, cv_ref, wk_ref, wv_ref, kpool_ref, vpool_ref,
                       imp_ref, e_ref, eg_ref, eye_ref, o_ref,
                       kbuf_ref, vbuf_ref, sem_ref, m_ref, l_ref, acc_ref, selT_ref, oc_ref, ow_ref,
                       *, pages, past, n_sel):
    rep = NSA_REP
    blocks_per_page = PAGE_SIZE // NSA_SEL_BLOCK
    stream = _PageStream(pt_ref, [kpool_ref, vpool_ref], [kbuf_ref, vbuf_ref], sem_ref, pages)
    slot = stream.advance()
    c = stream.c
    qn = qn_ref[0].astype(bf16)
    q8 = qn_ref[0, 0:HEAD_ROWS, :].astype(bf16)
    qpos = past
    n_rows = acc_ref.shape[0]
    lane = lax.broadcasted_iota(jnp.int32, (1, LANES), 1)
    tile4 = lambda a: jnp.concatenate([a] * rep, axis=1)

    @pl.when(c == 0)
    def _():
        n_cmp = ck_ref.shape[1]
        cmp_end = lax.broadcasted_iota(jnp.int32, (1, n_cmp), 1) * NSA_CMP_STRIDE + (NSA_CMP_LEN - 1)
        s = jnp.where(cmp_end <= qpos, _dot_nt(q8, ck_ref[0].astype(bf16)), MASKED)
        m = jnp.maximum(jnp.max(s, axis=-1, keepdims=True), ROW_MAX_INIT)
        e = jnp.exp(s - m)
        den = jnp.sum(e, axis=-1, keepdims=True)
        p_c = e / jnp.where(den > 0, den, 1.0)
        oc_ref[...] = _dot(p_c.astype(bf16), cv_ref[0].astype(bf16))
        row8 = lax.broadcasted_iota(jnp.int32, (HEAD_ROWS, 1), 0)
        p_grp = jnp.where(row8 < rep, jnp.sum(p_c[0:rep], axis=0, keepdims=True),
                          jnp.sum(p_c[rep:2 * rep], axis=0, keepdims=True))
        imp = _dot(p_grp.astype(bf16), imp_ref[...])
        n_sel_pad = imp.shape[1]
        jblk = lax.broadcasted_iota(jnp.int32, (1, n_sel_pad), 1)
        cur = qpos // NSA_SEL_BLOCK
        visible = jblk <= cur
        forced = (jblk == 0) | (jblk == cur) | (jblk == cur - 1)
        score = jnp.where(visible, jnp.where(forced, FORCED, imp), -FORCED)
        sel = _topk_mask(score, NSA_TOPN, jblk, n_sel) & visible
        sel_rows = jnp.concatenate([jnp.where(sel, 1.0, 0.0), jnp.zeros((LANES - HEAD_ROWS, n_sel_pad), f32)], axis=0)
        selT_ref[...] = _dot_nt(eye_ref[...], sel_rows.astype(bf16))
        n_w = wk_ref.shape[-1]
        wk = wk_ref[0].reshape(LANES, n_w).astype(bf16)
        wv = wv_ref[0].reshape(LANES, n_w).astype(bf16)
        kp = (past - n_w) + lax.broadcasted_iota(jnp.int32, (1, n_w), 1)
        wmask = (kp <= qpos) & (kp > qpos - NSA_WINDOW)
        kw_new = new_ref[0, 2:3, :].astype(bf16).astype(f32)
        vw_new = new_ref[0, 3:4, :].astype(bf16).astype(f32)
        s_n = jnp.sum(q8.astype(f32) * kw_new, axis=-1, keepdims=True)
        s_w = jnp.where(wmask, _dot(q8, wk), MASKED)
        m = jnp.maximum(jnp.max(s_w, axis=-1, keepdims=True), s_n)
        e = jnp.exp(s_w - m)
        e_n = jnp.exp(s_n - m)
        den = jnp.sum(e, axis=-1, keepdims=True) + e_n
        ow_ref[...] = (_dot_nt(e.astype(bf16), wv) + e_n.astype(bf16).astype(f32) * vw_new) / den
        tail = n_rows - (n_sel - 1)
        k_new = jnp.broadcast_to(new_ref[0, 0:1, :], (HEAD_ROWS, LANES)).astype(bf16)
        m_ref[pl.ds(n_sel - 1, tail), :] = jnp.broadcast_to(_dot_nt(k_new, qn)[0:1], (tail, LANES))
        l_ref[pl.ds(n_sel - 1, tail), :] = jnp.ones((tail, LANES), f32)
        acc_ref[pl.ds(n_sel - 1, tail), :] = jnp.broadcast_to(tile4(new_ref[0, 1:2, :]), (tail, acc_ref.shape[1]))

    def page_body(pg, carry):
        s = _dot_nt(_page_rows(kbuf_ref, slot, pg).astype(bf16), qn)
        vp = _page_rows(vbuf_ref, slot, pg)
        for i in range(blocks_per_page):
            rows = slice(i * NSA_SEL_BLOCK, (i + 1) * NSA_SEL_BLOCK)
            m, l, acc = _block_partial(s[rows], tile4(vp[rows]), e_ref)
            j = (c * pages + pg) * blocks_per_page + i
            m_ref[pl.ds(j, 1), :] = m
            l_ref[pl.ds(j, 1), :] = l
            acc_ref[pl.ds(j, 1), :] = acc
        return carry

    lax.fori_loop(0, pages, page_body, 0, unroll=4)

    @pl.when(c == stream.n_chunks - 1)
    def _():
        o_s = _merge_partials(selT_ref[0:n_rows, :] > 0.5, m_ref[...], l_ref[...], acc_ref[...], e_ref)
        wide = lambda o8: jnp.concatenate(
            [jnp.where(lane < HEAD_DIM, o8[r:r + 1], o8[rep + r:rep + r + 1]) for r in range(rep)], axis=1)
        gate = [_expand_lanes(g_ref[0], eg_ref.at[i])[0:1] for i in range(3)]
        o = gate[0] * wide(oc_ref[...]) + gate[1] * o_s + gate[2] * wide(ow_ref[...])
        o_ref[0] = jnp.broadcast_to(o, o_ref.shape[1:])


def _nsa_decode(q, gates, new_rows, c_k, c_v, win_k, win_v, k_pool, v_pool, page_table, *, pages=32):
    B = q.shape[0]
    G, R, d = NSA_GROUPS, NSA_REP, HEAD_DIM
    n_pages = page_table.shape[1]
    past = n_pages * PAGE_SIZE
    n_cmp = c_k.shape[1]
    n_sel = max(-(-(past + 1) // NSA_SEL_BLOCK), NSA_TOPN)
    n_sel_pad = _round_up(n_sel, LANES)
    n_rows = _round_up(n_sel, 8)
    assert past % NSA_SEL_BLOCK == 0 and n_pages % pages == 0 and past >= win_k.shape[-1] and G == 2
    assert k_pool.shape[1:] == (G, d, PAGE_SIZE) and win_k.shape[1:3] == (G, d)
    assert n_cmp == past // NSA_CMP_STRIDE and n_sel - 1 == past // NSA_SEL_BLOCK
    qg = jnp.einsum("bgrd,gk->bgrkd", q.reshape(B, G, R, d) * (d ** -0.5), jnp.eye(G, dtype=f32))
    qn = jnp.pad(qg.reshape(B, NSA_HEADS, LANES), ((0, 0), (0, LANES - NSA_HEADS), (0, 0)))
    g8 = jnp.broadcast_to(gates[:, None, :], (B, HEAD_ROWS, LANES))
    new8 = jnp.pad(new_rows, ((0, 0), (0, 8 - new_rows.shape[1]), (0, 0)))
    imp = np.zeros((n_cmp, n_sel_pad), np.float32)
    imp[:, :n_sel] = _nsa_static_maps(n_cmp, n_sel)
    e_np = np.zeros((LANES, NSA_Q_W), np.float32)
    eg_np = np.zeros((3, LANES, NSA_Q_W), np.float32)
    for g in range(G):
        for r in range(R):
            h = g * R + r
            cols = slice(r * LANES + g * d, r * LANES + (g + 1) * d)
            e_np[h, cols] = 1.0
            for i in range(3):
                eg_np[i, 3 * h + i, cols] = 1.0
    consts = [jnp.asarray(imp, bf16), jnp.asarray(e_np, bf16), jnp.asarray(eg_np, bf16),
              jnp.asarray(np.eye(n_sel_pad, dtype=np.float32), bf16)]
    per_b = lambda a: pl.BlockSpec((1,) + a.shape[1:], lambda b, c, pt: (b,) + (0,) * (a.ndim - 1))
    const = lambda a: pl.BlockSpec(a.shape, lambda b, c, pt: (0,) * a.ndim)
    anyspec = pl.BlockSpec(memory_space=pl.ANY)
    per_b_in = [qn, g8, new8, c_k, c_v, win_k, win_v]
    out = pl.pallas_call(
        functools.partial(_nsa_decode_kernel, pages=pages, past=past, n_sel=n_sel),
        grid_spec=pltpu.PrefetchScalarGridSpec(
            num_scalar_prefetch=1, grid=(B, n_pages // pages),
            in_specs=[per_b(a) for a in per_b_in] + [anyspec, anyspec] + [const(a) for a in consts],
            out_specs=pl.BlockSpec((1, 8, NSA_Q_W), lambda b, c, pt: (b, 0, 0)),
            scratch_shapes=[pltpu.VMEM((2, pages) + k_pool.shape[1:], f32), pltpu.VMEM((2, pages) + v_pool.shape[1:], f32),
                            pltpu.SemaphoreType.DMA((2, 2)),
                            pltpu.VMEM((n_rows, LANES), f32), pltpu.VMEM((n_rows, LANES), f32),
                            pltpu.VMEM((n_rows, NSA_Q_W), f32), pltpu.VMEM((n_sel_pad, LANES), f32),
                            pltpu.VMEM((HEAD_ROWS, LANES), f32), pltpu.VMEM((HEAD_ROWS, LANES), f32)]),
        out_shape=jax.ShapeDtypeStruct((B, 8, NSA_Q_W), f32),
        compiler_params=_params("arbitrary", "arbitrary"), name="nsa_decode")(
            page_table, *per_b_in, k_pool, v_pool, *consts)
    return out[:, 0]


def _mem_decode_kernel(q_ref, k_ref, v_ref, o_ref):
    d = MEM_HEAD_DIM
    s = _dot_nt(q_ref[0].astype(bf16), k_ref[0].astype(bf16)) * (d ** -0.5)
    e = jnp.exp(s - jnp.max(s, axis=-1, keepdims=True))
    p = e / jnp.sum(e, axis=-1, keepdims=True)
    o8 = _dot(p.astype(bf16), v_ref[0].astype(bf16))
    own = jnp.right_shift(lax.broadcasted_iota(jnp.int32, (1, o8.shape[1]), 1), d.bit_length() - 1) == \
        lax.broadcasted_iota(jnp.int32, (HEAD_ROWS, 1), 0)
    o_ref[0] = jnp.broadcast_to(jnp.sum(jnp.where(own, o8, 0.0), axis=0, keepdims=True), o_ref.shape[1:])


def _mem_decode(q, mk, mv):
    B, W = q.shape
    M = mk.shape[1]
    q8 = _head_rows(q.astype(f32), MEM_HEADS, f32)[:, :HEAD_ROWS]
    out = pl.pallas_call(
        _mem_decode_kernel, grid=(B,),
        in_specs=[pl.BlockSpec((1, HEAD_ROWS, W), lambda b: (b, 0, 0)), pl.BlockSpec((1, M, W), lambda b: (b, 0, 0)),
                  pl.BlockSpec((1, M, W), lambda b: (b, 0, 0))],
        out_specs=pl.BlockSpec((1, 8, W), lambda b: (b, 0, 0)),
        out_shape=jax.ShapeDtypeStruct((B, 8, W), f32),
        compiler_params=_params("parallel"), name="mem_decode")(q8, mk, mv)
    return out[:, 0]


def _prompt_layer(x, mem, lw):
    B, S, D = x.shape
    pos = jnp.arange(S, dtype=jnp.int32)
    pr = _project(x.reshape(B * S, D), pos, lw, head_order=NSA_HEAD_ORDER, q_dtype=bf16, q_scale=HEAD_DIM ** -0.5)
    seq = lambda name: pr[name].reshape(B, S, -1)
    Bm, M, _ = mem.shape
    kv = _matmul(mem.reshape(Bm * M, D), lw["w_mem_kv"].astype(bf16), gain=lw["norm_mem"], name="mem_kv")
    kv = kv.reshape(Bm, M, 2 * MEM_W)
    c_k = _compress(seq("k_cmp"), lw["cmp_w1_k"], lw["cmp_b1_k"], lw["cmp_w2_k"], chunks_per_step=128)
    c_v = _compress(seq("v_cmp"), lw["cmp_w1_v"], lw["cmp_b1_v"], lw["cmp_w2_v"], chunks_per_step=128)
    o_nsa = _nsa_prompt(seq("nsa_q"), seq("nsa_g"), c_k, c_v, seq("k_slc"), seq("v_slc"), seq("k_win"), seq("v_win"))
    o_moba = _moba_prompt(seq("moba_q"), seq("moba_k"), seq("moba_v"))
    o_mem = _mem_attn(seq("mem_q"), kv)
    merged = _merge(x.reshape(B * S, D), o_nsa.reshape(B * S, -1), o_moba.reshape(B * S, -1),
                    o_mem.reshape(B * S, -1), lw, nsa_head_order=NSA_HEAD_ORDER)
    wb = min(NSA_WINDOW, S)
    heads = lambda name, n, d: pr[name].reshape(B, S, n, d)
    state = (heads("k_cmp", NSA_GROUPS, HEAD_DIM), heads("v_cmp", NSA_GROUPS, HEAD_DIM),
             heads("k_slc", NSA_GROUPS, HEAD_DIM), heads("v_slc", NSA_GROUPS, HEAD_DIM),
             heads("k_win", NSA_GROUPS, HEAD_DIM)[:, S - wb:], heads("v_win", NSA_GROUPS, HEAD_DIM)[:, S - wb:],
             heads("moba_k", MOBA_HEADS, HEAD_DIM), heads("moba_v", MOBA_HEADS, HEAD_DIM),
             kv[:, :, :MEM_W].reshape(Bm, M, MEM_HEADS, MEM_HEAD_DIM),
             kv[:, :, MEM_W:].reshape(Bm, M, MEM_HEADS, MEM_HEAD_DIM))
    return merged, state


def _sample_layer(x, c_cmp_k, c_cmp_v, c_slc_k, c_slc_v, c_win_k, c_win_v,
                  c_moba_k, c_moba_v, c_mem_k, c_mem_v, page_table, lw):
    B, T, D = x.shape
    past_len = page_table.shape[1] * PAGE_SIZE
    assert T == 1 and past_len % NSA_CMP_STRIDE == 0
    pos = jnp.full((B,), past_len, jnp.int32)
    pr = _project(x.reshape(B, D), pos, lw, head_order=STD_HEAD_ORDER, q_dtype=f32, q_scale=1.0)
    flat = lambda a: a.reshape(a.shape[0], a.shape[1], -1)
    pages = _cache_pages
    c_k = _compress_paged(pages(c_cmp_k), page_table, lw["cmp_w1_k"], lw["cmp_b1_k"], lw["cmp_w2_k"])
    c_v = _compress_paged(pages(c_cmp_v), page_table, lw["cmp_w1_v"], lw["cmp_b1_v"], lw["cmp_w2_v"])
    new_rows = jnp.stack([pr["k_slc"], pr["v_slc"], pr["k_win"], pr["v_win"]], axis=1)
    o_nsa = _nsa_decode(pr["nsa_q"], pr["nsa_g"], new_rows, c_k, c_v, pages(c_win_k), pages(c_win_v),
                        pages(c_slc_k), pages(c_slc_v), page_table)
    o_moba = _moba_decode(pr["moba_q"], pr["moba_k"], pr["moba_v"], pages(c_moba_k), pages(c_moba_v), page_table)
    o_mem = _mem_decode(pr["mem_q"], flat(c_mem_k), flat(c_mem_v))
    merged = _merge(x.reshape(B, D), o_nsa.astype(bf16), o_moba.astype(bf16), o_mem.astype(bf16), lw,
                    nsa_head_order=NSA_HEAD_ORDER)
    heads = lambda name, n, d: pr[name].reshape(B, T, n, d)
    wb = c_win_k.shape[1]
    roll_in = lambda cache, name: jnp.concatenate([cache, heads(name, NSA_GROUPS, HEAD_DIM)], axis=1)[:, -wb:]
    state = (heads("k_cmp", NSA_GROUPS, HEAD_DIM), heads("v_cmp", NSA_GROUPS, HEAD_DIM),
             heads("k_slc", NSA_GROUPS, HEAD_DIM), heads("v_slc", NSA_GROUPS, HEAD_DIM),
             roll_in(c_win_k, "k_win"), roll_in(c_win_v, "v_win"),
             heads("moba_k", MOBA_HEADS, HEAD_DIM), heads("moba_v", MOBA_HEADS, HEAD_DIM))
    return merged, state


def kernel(x_prompt, x_sample, mem_prompt, cache_nsa_cmp_k, cache_nsa_cmp_v, cache_nsa_slc_k, cache_nsa_slc_v, cache_nsa_win_k, cache_nsa_win_v, cache_moba_k, cache_moba_v, cache_mem_k, cache_mem_v, page_table, norm_mix, norm_mem, w_in, w_mem_kv, cmp_w1_k, cmp_b1_k, cmp_w2_k, cmp_w1_v, cmp_b1_v, cmp_w2_v, w_nsa_out, w_moba_out, w_mem_out, w_merge_gate, b_merge_gate, w_out, norm_ffn, w_group, b_group, w_router, b_router, w_expert_gate, w_expert_up, w_expert_down, norm_final):
    depth = norm_mix.shape[0]
    xp, xs = x_prompt, x_sample
    Bp, S, D = xp.shape
    Bs, T, _ = xs.shape
    n_p = Bp * S
    p_layers, s_layers = [], []
    for l in range(depth):
        lw = dict(norm_mix=norm_mix[l], norm_mem=norm_mem[l], w_in=w_in[l], w_mem_kv=w_mem_kv[l],
                  cmp_w1_k=cmp_w1_k[l], cmp_b1_k=cmp_b1_k[l], cmp_w2_k=cmp_w2_k[l],
                  cmp_w1_v=cmp_w1_v[l], cmp_b1_v=cmp_b1_v[l], cmp_w2_v=cmp_w2_v[l],
                  w_nsa_out=w_nsa_out[l], w_moba_out=w_moba_out[l], w_mem_out=w_mem_out[l],
                  w_merge_gate=w_merge_gate[l], b_merge_gate=b_merge_gate[l], w_out=w_out[l],
                  norm_ffn=norm_ffn[l], w_group=w_group[l], b_group=b_group[l],
                  w_router=w_router[l], b_router=b_router[l], w_expert_gate=w_expert_gate[l],
                  w_expert_up=w_expert_up[l], w_expert_down=w_expert_down[l])
        (xp_mid, hp, lp), p_new = _prompt_layer(xp, mem_prompt, lw)
        (xs_mid, hs, ls), s_new = _sample_layer(xs, cache_nsa_cmp_k[l], cache_nsa_cmp_v[l], cache_nsa_slc_k[l],
                                                cache_nsa_slc_v[l], cache_nsa_win_k[l], cache_nsa_win_v[l],
                                                cache_moba_k[l], cache_moba_v[l], cache_mem_k[l], cache_mem_v[l],
                                                page_table, lw)
        picked, weight = _moe_rows([hp, hs], jnp.concatenate([lp, ls], axis=0), lw)
        last = l == depth - 1
        xp = _combine(xp_mid, picked[:n_p], weight[:n_p], norm_final, normalise=last).reshape(Bp, S, D)
        xs = _combine(xs_mid, picked[n_p:], weight[n_p:], norm_final, normalise=last).reshape(Bs, T, D)
        p_layers.append(p_new)
        s_layers.append(s_new)
    p_out = [jnp.stack(z) for z in zip(*p_layers)]
    s_out = [jnp.stack(z) for z in zip(*s_layers)]
    return (xp, xs, *p_out, *s_out)
```

```python
import functools

import numpy as np
import jax
import jax.numpy as jnp
from jax import lax
from jax.experimental import pallas as pl
from jax.experimental.pallas import tpu as pltpu

f32 = jnp.float32
bf16 = jnp.bfloat16

D_MODEL = 1024
PAGE_SIZE = 128
HEAD_DIM = 64
NSA_HEADS = 8
NSA_GROUPS = 2
NSA_REP = NSA_HEADS // NSA_GROUPS
NSA_CMP_LEN = 32
NSA_CMP_STRIDE = 16
NSA_CMP_HIDDEN = 256
NSA_SEL_BLOCK = 64
NSA_TOPN = 16
NSA_WINDOW = 512
MOBA_HEADS = 8
MOBA_BLOCK = 256
MOBA_TOPK = 3
MEM_HEADS = 4
MEM_HEAD_DIM = 128
N_BRANCHES = 3
N_GROUPS = 4
EXPERTS_PER_GROUP = 8
N_EXPERTS = N_GROUPS * EXPERTS_PER_GROUP
TOP_K_IN_GROUP = 2
EXPERT_FF = 512
ROPE_THETA = 10000.0
NORM_EPS = 1e-6

NSA_Q_W = NSA_HEADS * HEAD_DIM
NSA_KV_W = NSA_GROUPS * HEAD_DIM
NSA_GATE_W = NSA_HEADS * 3
MOBA_W = MOBA_HEADS * HEAD_DIM
MEM_W = MEM_HEADS * MEM_HEAD_DIM
PROJ_WIDTH = NSA_Q_W + NSA_GATE_W + 6 * NSA_KV_W + 3 * MOBA_W + MEM_W

LANES = 128
VMEM_LIMIT = 56 << 20
MASKED = -1e30
ROW_MAX_INIT = -1e20
FORCED = 1e30
MOE_ROWS = 256
HEAD_ROWS = 16

assert NSA_KV_W == LANES and 2 * HEAD_DIM == LANES and MEM_HEAD_DIM == LANES
NSA_HEAD_ORDER = tuple(g * NSA_REP + r for r in range(NSA_REP) for g in range(NSA_GROUPS))
STD_HEAD_ORDER = tuple(range(NSA_HEADS))


def _round_up(n, m):
    return -(-n // m) * m


def _params(*sem):
    return pltpu.CompilerParams(dimension_semantics=sem, vmem_limit_bytes=VMEM_LIMIT)


def _rms(xf, gain):
    return xf * lax.rsqrt(jnp.mean(xf * xf, axis=-1, keepdims=True) + NORM_EPS) * gain


def _dot(a, b):
    return jnp.dot(a, b, preferred_element_type=f32)


def _dot_nt(a, b):
    return lax.dot_general(a, b, (((1,), (1,)), ((), ())), preferred_element_type=f32)


def _split_hi_lo(a):
    hi = a.astype(bf16)
    lo = (a - hi.astype(f32)).astype(bf16)
    return hi, lo


def _mm_kernel(x_ref, w_ref, o_ref):
    o_ref[...] = _dot(x_ref[...].astype(bf16), w_ref[...]).astype(o_ref.dtype)


def _norm_mm_kernel(x_ref, g_ref, w_ref, o_ref, xn_ref):
    @pl.when(pl.program_id(1) == 0)
    def _():
        xn_ref[...] = _rms(x_ref[...], g_ref[...]).astype(bf16)

    o_ref[...] = _dot(xn_ref[...], w_ref[...]).astype(o_ref.dtype)


def _matmul(x, w, *, gain=None, out_dtype=f32, tm=512, tn=512, name="matmul"):
    M, K = x.shape
    N = w.shape[1]
    tm = min(tm, M)
    tn = max(t for t in range(LANES, min(tn, N) + 1, LANES) if N % t == 0)
    assert M % tm == 0 and N % LANES == 0, (M, N, tm, tn)
    grid = (M // tm, N // tn)
    x_spec = pl.BlockSpec((tm, K), lambda i, j: (i, 0))
    w_spec = pl.BlockSpec((K, tn), lambda i, j: (0, j))
    o_spec = pl.BlockSpec((tm, tn), lambda i, j: (i, j))
    out_shape = jax.ShapeDtypeStruct((M, N), out_dtype)
    if gain is None:
        return pl.pallas_call(_mm_kernel, grid=grid, in_specs=[x_spec, w_spec], out_specs=o_spec,
                              out_shape=out_shape, compiler_params=_params("parallel", "arbitrary"),
                              name=name)(x, w)
    return pl.pallas_call(
        _norm_mm_kernel, grid=grid,
        in_specs=[x_spec, pl.BlockSpec((1, K), lambda i, j: (0, 0)), w_spec],
        out_specs=o_spec, out_shape=out_shape, scratch_shapes=[pltpu.VMEM((tm, K), bf16)],
        compiler_params=_params("parallel", "arbitrary"), name=name)(x, gain.reshape(1, K), w)


_PROJ_SEGMENTS = (("nsa_q", 4, True), ("k_cmp", 1, True), ("v_cmp", 1, False), ("k_slc", 1, True),
                  ("v_slc", 1, False), ("k_win", 1, True), ("v_win", 1, False), ("moba_q", 4, True),
                  ("moba_k", 4, True), ("moba_v", 4, False), ("mem_q", 4, False), ("nsa_g", 1, False))
_PROJ_BLOCKS = sum(n for _, n, _ in _PROJ_SEGMENTS)


def _proj_outputs(dim_major):
    outs = []
    for name, _, _ in _PROJ_SEGMENTS:
        if name not in dim_major or name + "+" in dim_major:
            outs.append((name, False))
        if name in dim_major:
            outs.append((name, True))
    return outs


def _proj_kernel(x_ref, g_ref, w_ref, cos_ref, sin_ref, *o_refs, q_scale, dim_major):
    refs = dict(zip(_proj_outputs(dim_major), o_refs))
    xn = _rms(x_ref[...], g_ref[...]).astype(bf16)
    cos = cos_ref[...]
    sin = sin_ref[...]
    lane = lax.broadcasted_iota(jnp.int32, (1, LANES), 1)
    first_half = (lane & (HEAD_DIM // 2)) == 0

    def rope(a):
        partner = jnp.where(first_half, pltpu.roll(a, LANES - HEAD_DIM // 2, 1), pltpu.roll(a, HEAD_DIM // 2, 1))
        return a * cos + partner * sin

    c = 0
    for name, n, rotary in _PROJ_SEGMENTS:
        acc = _dot(xn, w_ref[:, c * LANES:(c + n) * LANES])
        c += n
        for r in range(n):
            a = acc[:, r * LANES:(r + 1) * LANES]
            if rotary:
                a = rope(a)
            if name == "nsa_q":
                a = a * q_scale
            if name == "nsa_g":
                a = jax.nn.sigmoid(a)
            if (name, False) in refs:
                o_ref = refs[name, False]
                o_ref[:, r * LANES:(r + 1) * LANES] = a.astype(o_ref.dtype)
            if (name, True) in refs:
                refs[name, True][0, r * LANES:(r + 1) * LANES, :] = a.T


def _proj_weight(w_in, head_order):
    D = w_in.shape[0]
    q = w_in[:, :NSA_Q_W].reshape(D, NSA_HEADS, HEAD_DIM)[:, np.array(head_order)].reshape(D, NSA_Q_W)
    gates = jnp.pad(w_in[:, NSA_Q_W:NSA_Q_W + NSA_GATE_W], ((0, 0), (0, LANES - NSA_GATE_W)))
    return jnp.concatenate([q, w_in[:, NSA_Q_W + NSA_GATE_W:], gates], axis=1).astype(bf16)


def _rope_tables(pos):
    half = HEAD_DIM // 2
    inv_freq = ROPE_THETA ** (-jnp.arange(half, dtype=f32) / half)
    ang = pos.astype(f32)[:, None] * inv_freq[None, :]
    cos, sin = jnp.cos(ang), jnp.sin(ang)
    reps = LANES // HEAD_DIM
    return jnp.tile(jnp.concatenate([cos, cos], axis=1), (1, reps)), jnp.tile(jnp.concatenate([-sin, sin], axis=1), (1, reps))


def _project(x, pos, lw, *, head_order, q_dtype, q_scale, dim_major=(), tm=512):
    M, D = x.shape
    P = pos.shape[0]
    tm = min(tm, M, P)
    assert M % tm == 0 and P % tm == 0
    cos, sin = _rope_tables(pos)
    dtypes = dict(nsa_q=q_dtype, mem_q=bf16)
    width = {name: n * LANES for name, n, _ in _PROJ_SEGMENTS}
    n_pos = P // tm
    outputs = _proj_outputs(dim_major)
    out_shape, out_specs = [], []
    for name, transposed in outputs:
        if transposed:
            out_shape.append(jax.ShapeDtypeStruct((M // P, width[name], P), f32))
            out_specs.append(pl.BlockSpec((1, width[name], tm), lambda i: (i // n_pos, 0, i % n_pos)))
        else:
            out_shape.append(jax.ShapeDtypeStruct((M, width[name]), dtypes.get(name, f32)))
            out_specs.append(pl.BlockSpec((tm, width[name]), lambda i: (i, 0)))
    outs = pl.pallas_call(
        functools.partial(_proj_kernel, q_scale=q_scale, dim_major=dim_major), grid=(M // tm,),
        in_specs=[pl.BlockSpec((tm, D), lambda i: (i, 0)), pl.BlockSpec((1, D), lambda i: (0, 0)),
                  pl.BlockSpec((D, _PROJ_BLOCKS * LANES), lambda i: (0, 0)),
                  pl.BlockSpec((tm, LANES), lambda i: (i % n_pos, 0)),
                  pl.BlockSpec((tm, LANES), lambda i: (i % n_pos, 0))],
        out_specs=out_specs, out_shape=out_shape, compiler_params=_params("parallel"),
        name="proj_in")(x, lw["norm_mix"].reshape(1, D), _proj_weight(lw["w_in"], head_order), cos, sin)
    return {name + ("_t" if transposed else ""): o for (name, transposed), o in zip(outputs, outs)}


def _compress_kernel(rows_ref, nxt_ref, w1_ref, b1_ref, w2_ref, o_ref, buf_ref, *, n_chunks):
    stride = NSA_CMP_STRIDE
    n_rows = n_chunks * stride
    buf_ref[0:n_rows, :] = rows_ref[0]
    buf_ref[n_rows:n_rows + stride, :] = nxt_ref[0]
    acc = jnp.zeros((n_chunks, w1_ref.shape[-1]), f32)
    for p in range(0, NSA_CMP_LEN, 2):
        lhs = jnp.concatenate([buf_ref[pl.ds(p, n_chunks, stride=stride), :],
                               buf_ref[pl.ds(p + 1, n_chunks, stride=stride), :]], axis=1)
        acc = acc + _dot(lhs.astype(bf16), w1_ref[p // 2])
    hid = jax.nn.gelu(acc + b1_ref[...])
    o_ref[0] = _dot(hid.astype(bf16), w2_ref[...])


def _compress(rows, w1, b1, w2, *, chunks_per_step):
    B, L, W = rows.shape
    G, dh, stride = NSA_GROUPS, HEAD_DIM, NSA_CMP_STRIDE
    n_total = L // stride
    C = min(chunks_per_step, n_total)
    assert W == G * dh and L % stride == 0 and n_total % C == 0 and NSA_CMP_LEN == 2 * stride
    w1bd, b1bd, w2bd = _compress_weights(w1, b1, w2)
    n_steps = n_total // C
    last_blk = L // stride - 1
    return pl.pallas_call(
        functools.partial(_compress_kernel, n_chunks=C), grid=(B, n_steps),
        in_specs=[pl.BlockSpec((1, C * stride, W), lambda b, i: (b, i, 0)),
                  pl.BlockSpec((1, stride, W), lambda b, i: (b, jnp.minimum((i + 1) * C, last_blk), 0)),
                  pl.BlockSpec(w1bd.shape, lambda b, i: (0, 0, 0)),
                  pl.BlockSpec(b1bd.shape, lambda b, i: (0, 0)),
                  pl.BlockSpec(w2bd.shape, lambda b, i: (0, 0))],
        out_specs=pl.BlockSpec((1, C, W), lambda b, i: (b, i, 0)),
        out_shape=jax.ShapeDtypeStruct((B, n_total, W), f32),
        scratch_shapes=[pltpu.VMEM((C * stride + stride, W), f32)],
        compiler_params=_params("parallel", "arbitrary"), name="compress")(rows, rows, w1bd, b1bd, w2bd)


def _qk(q, kt, dim_major):
    return _dot(q, kt) if dim_major else _dot_nt(q, kt)


def _pv(p, vt, dim_major):
    return _dot_nt(p, vt) if dim_major else _dot(p, vt)


def _flash_tile(q, kt, vt, mask, m, l, acc, rep, dim_major=False):
    tq, tk = mask.shape
    s = _qk(q, kt.astype(bf16), dim_major).reshape(rep, tq, tk)
    s = jnp.where(mask[None], s, MASKED)
    m_new = jnp.maximum(m, jnp.max(s, axis=-1, keepdims=True))
    p = jnp.exp(s - m_new)
    alpha = jnp.exp(m - m_new)
    l = alpha * l + jnp.sum(p, axis=-1, keepdims=True)
    pv = _pv(p.reshape(rep * tq, tk).astype(bf16), vt.astype(bf16), dim_major)
    acc = alpha.reshape(rep * tq, 1) * acc + pv
    return m_new, l, acc


def _softmax_tile(q, kt, vt, mask, rep, dim_major=False):
    tq, tk = mask.shape
    s = jnp.where(mask[None], _qk(q, kt, dim_major).reshape(rep, tq, tk), MASKED)
    m = jnp.maximum(jnp.max(s, axis=-1, keepdims=True), ROW_MAX_INIT)
    e = jnp.exp(s - m)
    den = jnp.sum(e, axis=-1, keepdims=True)
    p = e / jnp.where(den > 0, den, 1.0)
    return p, _pv(p.reshape(rep * tq, tk).astype(bf16), vt, dim_major)


def _flash_init(rep, tq):
    return (jnp.full((rep, tq, 1), ROW_MAX_INIT, f32), jnp.zeros((rep, tq, 1), f32),
            jnp.zeros((rep * tq, LANES), f32))


def _flash_finish(l, acc):
    l = l.reshape(acc.shape[0], 1)
    return acc / jnp.where(l > 0, l, 1.0)


def _topk_mask(score, k, col_index, n_real):
    rank = jnp.zeros(score.shape, jnp.int32)
    for i in range(n_real):
        si = score[:, i:i + 1]
        ahead = (si > score) | ((si == score) & (col_index > i))
        rank = rank + jnp.where(ahead, 1, 0)
    return rank < k


def _mask_to_bf16(mask):
    return jnp.where(mask, 1.0, 0.0).astype(bf16)


def _nsa_prompt_kernel(q_ref, g_ref, ck_ref, cv_ref, k_ref, v_ref, wk_ref, wv_ref,
                       imp_ref, esel_ref, o_ref, selexp_ref, *, tq, tk, n_sel):
    rep = NSA_REP
    s0 = pl.program_id(1) * tq
    qpos = s0 + lax.broadcasted_iota(jnp.int32, (tq, 1), 0)
    n_cmp = ck_ref.shape[1]
    cmp_end = lax.broadcasted_iota(jnp.int32, (1, n_cmp), 1) * NSA_CMP_STRIDE + (NSA_CMP_LEN - 1)
    cmp_mask = cmp_end <= qpos
    lane = lax.broadcasted_iota(jnp.int32, (1, LANES), 1)
    cur = jnp.right_shift(qpos, NSA_SEL_BLOCK.bit_length() - 1)
    visible = lane <= cur
    forced = (lane == 0) | (lane == cur) | (lane == cur - 1)
    ck = ck_ref[0].astype(bf16)
    cv = cv_ref[0].astype(bf16)
    groups = range(NSA_GROUPS)

    def group_query(g):
        in_group = (lane >= g * HEAD_DIM) & (lane < (g + 1) * HEAD_DIM)
        return jnp.concatenate(
            [jnp.where(in_group, q_ref[0, :, r * LANES:(r + 1) * LANES], jnp.zeros((), bf16)) for r in range(rep)],
            axis=0)

    q = [group_query(g) for g in groups]

    o_c = []
    for g in groups:
        p_c, o = _softmax_tile(q[g], ck, cv, cmp_mask, rep)
        o_c.append(o)
        imp = _dot(jnp.sum(p_c, axis=0).astype(bf16), imp_ref[...])
        score = jnp.where(visible, jnp.where(forced, FORCED, imp), -FORCED)
        sel = _topk_mask(score, NSA_TOPN, lane, n_sel) & visible
        selexp_ref[g] = _dot(_mask_to_bf16(sel), esel_ref[...])

    def slc_body(t, carry):
        k0 = pl.multiple_of(t * tk, tk)
        causal = (k0 + lax.broadcasted_iota(jnp.int32, (1, tk), 1)) <= qpos
        kt = k_ref[0, :, pl.ds(k0, tk)].astype(bf16)
        vt = v_ref[0, :, pl.ds(k0, tk)].astype(bf16)
        return tuple(_flash_tile(q[g], kt, vt, (selexp_ref[g, :, pl.ds(k0, tk)] > 0.5) & causal, *carry[g], rep,
                                 dim_major=True) for g in groups)

    slc = lax.fori_loop(0, (s0 + tq + tk - 1) // tk, slc_body, tuple(_flash_init(rep, tq) for g in groups))
    o_s = [_flash_finish(l, acc) for _, l, acc in slc]

    n_win = NSA_WINDOW + tq
    w0 = pl.multiple_of(jnp.maximum(s0 - NSA_WINDOW, 0), tq)
    kpos = w0 + lax.broadcasted_iota(jnp.int32, (1, n_win), 1)
    wmask = (kpos <= qpos) & (kpos > qpos - NSA_WINDOW)
    wk = wk_ref[0, :, pl.ds(w0, n_win)].astype(bf16)
    wv = wv_ref[0, :, pl.ds(w0, n_win)].astype(bf16)
    o_w = [_softmax_tile(q[g], wk, wv, wmask, rep, dim_major=True)[1] for g in groups]

    for r in range(rep):
        halves = []
        for g in groups:
            c = 3 * (g * rep + r)
            rows = slice(r * tq, (r + 1) * tq)
            halves.append(g_ref[0, :, c:c + 1] * o_c[g][rows] + g_ref[0, :, c + 1:c + 2] * o_s[g][rows]
                          + g_ref[0, :, c + 2:c + 3] * o_w[g][rows])
        o_ref[0, :, r * LANES:(r + 1) * LANES] = jnp.where(lane < HEAD_DIM, halves[0], halves[1]).astype(o_ref.dtype)


def _nsa_static_maps(n_cmp_pad, n_sel):
    ratio_c = NSA_CMP_LEN // NSA_CMP_STRIDE
    ratio_s = NSA_SEL_BLOCK // NSA_CMP_STRIDE
    nc = n_cmp_pad - ratio_c + 1
    mat = np.zeros((n_cmp_pad, n_sel), np.float32)
    j = np.arange(n_sel)
    for mm in range(ratio_s):
        for n in range(ratio_c):
            i = ratio_s * j + mm - n
            ok = (i >= 0) & (i < nc)
            np.add.at(mat, (i[ok], j[ok]), 1.0)
    return mat


def _block_expand(n_blocks, block, total):
    k = np.arange(total)
    return (k[None, :] // block == np.arange(n_blocks)[:, None]).astype(np.float32)


def _nsa_prompt(q, gates, c_k, c_v, k_slc, v_slc, k_win, v_win, *, tq=128, tk=256):
    B, S, _ = q.shape
    n_cmp = S // NSA_CMP_STRIDE
    n_sel = max(S // NSA_SEL_BLOCK, NSA_TOPN)
    assert S % tq == 0 and S % tk == 0 and NSA_WINDOW % tq == 0 and S >= NSA_WINDOW + tq
    assert n_sel * NSA_SEL_BLOCK == S and n_sel <= LANES and c_k.shape[1] == n_cmp
    imp = jnp.asarray(np.pad(_nsa_static_maps(n_cmp, n_sel), ((0, 0), (0, LANES - n_sel))), bf16)
    esel = jnp.asarray(_block_expand(LANES, NSA_SEL_BLOCK, S), bf16)
    assert k_slc.shape == (B, LANES, S) and k_win.shape == (B, LANES, S)
    per_b = lambda rows: pl.BlockSpec((1, rows, LANES), lambda b, i: (b, 0, 0))
    seq = pl.BlockSpec((1, LANES, S), lambda b, i: (b, 0, 0))
    full = lambda shape: pl.BlockSpec(shape, lambda b, i: (0,) * len(shape))
    return pl.pallas_call(
        functools.partial(_nsa_prompt_kernel, tq=tq, tk=tk, n_sel=n_sel),
        grid=(B, S // tq),
        in_specs=[pl.BlockSpec((1, tq, NSA_Q_W), lambda b, i: (b, i, 0)),
                  pl.BlockSpec((1, tq, LANES), lambda b, i: (b, i, 0)),
                  per_b(n_cmp), per_b(n_cmp), seq, seq, seq, seq,
                  full((n_cmp, LANES)), full((LANES, S))],
        out_specs=pl.BlockSpec((1, tq, NSA_Q_W), lambda b, i: (b, i, 0)),
        out_shape=jax.ShapeDtypeStruct((B, S, NSA_Q_W), bf16),
        scratch_shapes=[pltpu.VMEM((NSA_GROUPS, tq, S), f32)],
        compiler_params=_params("parallel", "arbitrary"),
        name="nsa_prompt")(q, gates, c_k, c_v, k_slc, v_slc, k_win, v_win, imp, esel)


def _moba_prompt_kernel(q_ref, k_ref, v_ref, ablk_ref, o_ref, means_ref, *, tq, n_blk):
    tk = MOBA_BLOCK
    n_pairs = q_ref.shape[-1] // LANES
    qi = pl.program_id(1)
    s0 = qi * tq
    cur = s0 // MOBA_BLOCK
    qpos = s0 + (lax.broadcasted_iota(jnp.int32, (2 * tq, 1), 0) & (tq - 1))
    lane = lax.broadcasted_iota(jnp.int32, (1, LANES), 1)
    low = lane < HEAD_DIM
    past = lane < cur

    @pl.when(qi == 0)
    def _():
        for j in range(n_pairs):
            kj = k_ref[0, j * LANES:(j + 1) * LANES, :]
            k_hi, k_lo = _split_hi_lo(kj)
            k_lo2 = (kj - k_hi.astype(f32) - k_lo.astype(f32)).astype(bf16)
            means_ref[j] = (_dot_nt(k_hi, ablk_ref[...]) + _dot_nt(k_lo, ablk_ref[...])
                            + _dot_nt(k_lo2, ablk_ref[...]))

    pairs = range(n_pairs)
    cols = [slice(j * LANES, (j + 1) * LANES) for j in pairs]
    q, sel = [], []
    for j in pairs:
        qj = q_ref[0, :, cols[j]]
        qf = jnp.concatenate([jnp.where(low, qj, 0.0), jnp.where(low, 0.0, qj)], axis=0)
        q.append((qf * (HEAD_DIM ** -0.5)).astype(bf16))
        gate = _dot(q[j], means_ref[j].astype(bf16))
        score = jnp.where(past, gate, -FORCED)
        picked = (_topk_mask(score, MOBA_TOPK, lane, n_blk) & past) | (lane == cur)
        sel.append(jnp.where(picked, 1.0, 0.0))

    def body(t, carry):
        k0 = pl.multiple_of(t * tk, tk)
        causal = (k0 + lax.broadcasted_iota(jnp.int32, (1, tk), 1)) <= qpos
        out = []
        for j in pairs:
            block_on = jnp.max(jnp.where(lane == t, sel[j], 0.0), axis=1, keepdims=True) > 0.5
            out.append(_flash_tile(q[j], k_ref[0, cols[j], pl.ds(k0, tk)], v_ref[0, cols[j], pl.ds(k0, tk)],
                                   block_on & causal, *carry[j], 1, dim_major=True))
        return tuple(out)

    state = lax.fori_loop(0, cur + 1, body, tuple(_flash_init(1, 2 * tq) for j in pairs))
    for j in pairs:
        o = _flash_finish(state[j][1], state[j][2])
        o_ref[0, :, cols[j]] = jnp.where(low, o[:tq], o[tq:]).astype(o_ref.dtype)


def _moba_prompt(q, k, v, *, tq=256):
    B, S, W = q.shape
    n_blk = S // MOBA_BLOCK
    assert S % MOBA_BLOCK == 0 and MOBA_BLOCK % tq == 0 and tq & (tq - 1) == 0
    assert MOBA_TOPK <= n_blk <= LANES and W % LANES == 0 and k.shape == (B, W, S)
    ablk = jnp.asarray(_block_expand(LANES, MOBA_BLOCK, S) / MOBA_BLOCK, bf16)
    per_b = pl.BlockSpec((1, W, S), lambda b, i: (b, 0, 0))
    return pl.pallas_call(
        functools.partial(_moba_prompt_kernel, tq=tq, n_blk=n_blk),
        grid=(B, S // tq),
        in_specs=[pl.BlockSpec((1, tq, W), lambda b, i: (b, i, 0)), per_b, per_b,
                  pl.BlockSpec((LANES, S), lambda b, i: (0, 0))],
        out_specs=pl.BlockSpec((1, tq, W), lambda b, i: (b, i, 0)),
        out_shape=jax.ShapeDtypeStruct((B, S, W), bf16),
        scratch_shapes=[pltpu.VMEM((W // LANES, LANES, LANES), f32)],
        compiler_params=_params("parallel", "arbitrary"),
        name="moba_prompt")(q, k, v, ablk)


def _mem_attn_kernel(q_ref, kv_ref, o_ref):
    d = MEM_HEAD_DIM
    for h in range(MEM_HEADS):
        k = kv_ref[0, :, h * d:(h + 1) * d].astype(bf16)
        v = kv_ref[0, :, MEM_W + h * d:MEM_W + (h + 1) * d].astype(bf16)
        s = _dot_nt(q_ref[0, :, h * d:(h + 1) * d], k) * (d ** -0.5)
        e = jnp.exp(s - jnp.max(s, axis=-1, keepdims=True))
        p = e / jnp.sum(e, axis=-1, keepdims=True)
        o_ref[0, :, h * d:(h + 1) * d] = _dot(p.astype(bf16), v).astype(o_ref.dtype)


def _mem_attn(q, kv, *, tq=512):
    B, T, W = q.shape
    M = kv.shape[1]
    tq = min(tq, T)
    assert T % tq == 0
    return pl.pallas_call(
        _mem_attn_kernel, grid=(B, T // tq),
        in_specs=[pl.BlockSpec((1, tq, W), lambda b, i: (b, i, 0)),
                  pl.BlockSpec((1, M, 2 * W), lambda b, i: (b, 0, 0))],
        out_specs=pl.BlockSpec((1, tq, W), lambda b, i: (b, i, 0)),
        out_shape=jax.ShapeDtypeStruct((B, T, W), bf16),
        compiler_params=_params("parallel", "arbitrary"), name="mem_attn")(q, kv)


def _merge_kernel(x_ref, gmix_ref, on_ref, om_ref, oe_ref, wg_ref, bg_ref, wn_ref, wm_ref, we_ref, wo_ref,
                  gffn_ref, wr_ref, br_ref, xmid_ref, h_ref, logit_ref):
    D = x_ref.shape[1]
    xf = x_ref[...]
    hn = _rms(xf, gmix_ref[...]).astype(bf16)
    mixed = None
    for i, (o_ref, w_ref) in enumerate(((on_ref, wn_ref), (om_ref, wm_ref), (oe_ref, we_ref))):
        gate = jax.nn.sigmoid(_dot(hn, wg_ref[:, i * D:(i + 1) * D]) + bg_ref[:, i * D:(i + 1) * D])
        term = gate * _dot(o_ref[...], w_ref[...])
        mixed = term if mixed is None else mixed + term
    xm = xf + _dot(mixed.astype(bf16), wo_ref[...])
    xmid_ref[...] = xm
    y = _rms(xm, gffn_ref[...])
    h_ref[...] = y.astype(bf16)
    logit_ref[...] = _dot(y.astype(bf16), wr_ref[...]) + br_ref[...]


def _merge(x, o_nsa, o_moba, o_mem, lw, *, nsa_head_order, tm=512):
    M, D = x.shape
    tm = min(tm, M)
    assert M % tm == 0
    n_route = N_GROUPS + N_EXPERTS
    w_route = jnp.pad(jnp.concatenate([lw["w_group"], lw["w_router"]], axis=1), ((0, 0), (0, LANES - n_route)))
    b_route = jnp.pad(jnp.concatenate([lw["b_group"], lw["b_router"]]), (0, LANES - n_route)).reshape(1, LANES)
    w_nsa = lw["w_nsa_out"].reshape(NSA_HEADS, HEAD_DIM, D)[np.array(nsa_head_order)].reshape(NSA_Q_W, D)
    row = lambda w: pl.BlockSpec((tm, w), lambda i: (i, 0))
    const = lambda a: pl.BlockSpec(a.shape, lambda i: (0,) * a.ndim)
    consts = [lw["w_merge_gate"].astype(bf16), lw["b_merge_gate"].reshape(1, -1), w_nsa.astype(bf16),
              lw["w_moba_out"].astype(bf16), lw["w_mem_out"].astype(bf16), lw["w_out"].astype(bf16),
              lw["norm_ffn"].reshape(1, D), w_route.astype(bf16), b_route]
    gmix = lw["norm_mix"].reshape(1, D)
    return pl.pallas_call(
        _merge_kernel, grid=(M // tm,),
        in_specs=[row(D), const(gmix), row(NSA_Q_W), row(MOBA_W), row(MEM_W)] + [const(a) for a in consts],
        out_specs=[row(D), row(D), row(LANES)],
        out_shape=[jax.ShapeDtypeStruct((M, D), f32), jax.ShapeDtypeStruct((M, D), bf16),
                   jax.ShapeDtypeStruct((M, LANES), f32)],
        compiler_params=_params("parallel"), name="merge")(x, gmix, o_nsa, o_moba, o_mem, *consts)


def _expert_kernel(be_ref, x_ref, wg_ref, wu_ref, wd_ref, o_ref, wg_s, wu_s, wd_s):
    i = pl.program_id(0)

    @pl.when((i == 0) | (be_ref[i] != be_ref[jnp.maximum(i - 1, 0)]))
    def _():
        wg_s[...] = wg_ref[0].astype(bf16)
        wu_s[...] = wu_ref[0].astype(bf16)
        wd_s[...] = wd_ref[0].astype(bf16)

    x = x_ref[...]
    a = _dot(x, wg_s[...])
    u = _dot(x, wu_s[...])
    mid = (a * jax.nn.sigmoid(a)) * u
    o_ref[...] = _dot(mid.astype(bf16), wd_s[...])


def _expert_ffn(blk_expert, xs, w_gate, w_up, w_down):
    cap, D = xs.shape
    F = w_gate.shape[-1]
    return pl.pallas_call(
        _expert_kernel,
        grid_spec=pltpu.PrefetchScalarGridSpec(
            num_scalar_prefetch=1, grid=(cap // MOE_ROWS,),
            in_specs=[pl.BlockSpec((MOE_ROWS, D), lambda i, be: (i, 0)),
                      pl.BlockSpec((1, D, F), lambda i, be: (be[i], 0, 0)),
                      pl.BlockSpec((1, D, F), lambda i, be: (be[i], 0, 0)),
                      pl.BlockSpec((1, F, D), lambda i, be: (be[i], 0, 0))],
            out_specs=pl.BlockSpec((MOE_ROWS, D), lambda i, be: (i, 0)),
            scratch_shapes=[pltpu.VMEM((D, F), bf16), pltpu.VMEM((D, F), bf16), pltpu.VMEM((F, D), bf16)]),
        out_shape=jax.ShapeDtypeStruct((cap, D), f32),
        compiler_params=_params("arbitrary"), name="expert_ffn")(blk_expert, xs, w_gate, w_up, w_down)


def _moe_rows(h_parts, logits, lw):
    n_tok, D = logits.shape[0], h_parts[0].shape[1]
    n_route = N_GROUPS + N_EXPERTS
    g_logits = logits[:, :N_GROUPS]
    g_prob = jax.nn.softmax(g_logits, axis=-1)
    grp = jnp.argmax(g_logits, axis=-1).astype(jnp.int32)
    e_logits = logits[:, N_GROUPS:n_route].reshape(-1, N_GROUPS, EXPERTS_PER_GROUP)
    e_in = jnp.take_along_axis(e_logits, grp[:, None, None], axis=1)[:, 0]
    top_p, top_i = lax.top_k(jax.nn.softmax(e_in, axis=-1), TOP_K_IN_GROUP)
    weight = top_p / jnp.sum(top_p, axis=-1, keepdims=True) * jnp.take_along_axis(g_prob, grp[:, None], axis=1)
    expert = grp[:, None] * EXPERTS_PER_GROUP + top_i.astype(jnp.int32)
    n_assign = n_tok * TOP_K_IN_GROUP
    n_blocks = -(-n_assign // MOE_ROWS) + N_EXPERTS
    cap = n_blocks * MOE_ROWS
    flat_e = expert.reshape(-1)
    order = jnp.argsort(flat_e, stable=True).astype(jnp.int32)
    rank = jnp.argsort(order).astype(jnp.int32)
    experts = jnp.arange(N_EXPERTS, dtype=jnp.int32)
    sort_end = jnp.sum((flat_e[:, None] <= experts[None, :]).astype(jnp.int32), axis=0)
    counts = sort_end - jnp.concatenate([jnp.zeros((1,), jnp.int32), sort_end[:-1]])
    sort_start = sort_end - counts
    padded = (counts + MOE_ROWS - 1) // MOE_ROWS * MOE_ROWS
    pad_end = jnp.cumsum(padded)
    pad_start = pad_end - padded
    blk_first = jnp.arange(n_blocks, dtype=jnp.int32) * MOE_ROWS
    blk_expert = jnp.minimum(jnp.sum((pad_end[None, :] <= blk_first[:, None]).astype(jnp.int32), axis=1),
                             N_EXPERTS - 1)
    slot_e = jnp.repeat(blk_expert, MOE_ROWS)
    slot_r = jnp.arange(cap, dtype=jnp.int32) - pad_start[slot_e]
    slot_pair = order[jnp.clip(sort_start[slot_e] + slot_r, 0, n_assign - 1)]
    slot_tok = jnp.where(slot_r < counts[slot_e], slot_pair // TOP_K_IN_GROUP, n_tok)
    dest = pad_start[flat_e] + rank - sort_start[flat_e]
    h_pad = jnp.concatenate(list(h_parts) + [jnp.zeros((1, D), h_parts[0].dtype)], axis=0)
    ys = _expert_ffn(blk_expert, h_pad[slot_tok], lw["w_expert_gate"], lw["w_expert_up"], lw["w_expert_down"])
    return ys, dest.reshape(n_tok, TOP_K_IN_GROUP), weight


def _combine_kernel(x_ref, p0_ref, p1_ref, w_ref, g_ref, o_ref, *, normalise):
    y = x_ref[...] + (w_ref[:, 0:1] * p0_ref[...] + w_ref[:, 1:2] * p1_ref[...])
    o_ref[...] = _rms(y, g_ref[...]) if normalise else y


def _combine(x, ys, dest, weight, gain, *, normalise, tm=512):
    M, D = x.shape
    tm = min(tm, M)
    assert M % tm == 0 and TOP_K_IN_GROUP == 2
    row = pl.BlockSpec((tm, D), lambda i: (i, 0))
    return pl.pallas_call(
        functools.partial(_combine_kernel, normalise=normalise), grid=(M // tm,),
        in_specs=[row, row, row, pl.BlockSpec((tm, 2), lambda i: (i, 0)), pl.BlockSpec((1, D), lambda i: (0, 0))],
        out_specs=row, out_shape=jax.ShapeDtypeStruct((M, D), f32),
        compiler_params=_params("parallel"), name="combine")(
            x, ys[dest[:, 0]], ys[dest[:, 1]], weight, gain.reshape(1, D))


def _cache_pages(cache):
    return jnp.transpose(cache, (0, 2, 3, 1))


def _page_rows(buf_ref, slot, page, heads=None):
    h0, h1 = heads if heads is not None else (0, buf_ref.shape[2])
    slab = buf_ref[slot, page, h0:h1]
    return slab.reshape((h1 - h0) * slab.shape[1], slab.shape[2]).T


class _PageStream:
    def __init__(self, pt_ref, pool_refs, buf_refs, sem_ref, pages_per_chunk):
        self.pt_ref, self.pools, self.bufs, self.sem = pt_ref, pool_refs, buf_refs, sem_ref
        self.P = pages_per_chunk
        self.n_chunks = pl.num_programs(1)
        self.b, self.c = pl.program_id(0), pl.program_id(1)
        self.step = self.b * self.n_chunks + self.c
        self.slot = self.step % 2

    def _copies(self, b, c, slot):
        out = []
        for k, (pool, buf) in enumerate(zip(self.pools, self.bufs)):
            for p in range(self.P):
                out.append(pltpu.make_async_copy(pool.at[self.pt_ref[b, c * self.P + p]], buf.at[slot, p],
                                                 self.sem.at[k, slot]))
        return out

    def extra_copies(self, b, c, slot):
        return []

    def _start(self, b, c, slot):
        for cp in self._copies(b, c, slot) + self.extra_copies(b, c, slot):
            cp.start()

    def advance(self):
        @pl.when(self.step == 0)
        def _():
            self._start(self.b, self.c, self.slot)

        last_c = self.c == self.n_chunks - 1
        nb = jnp.where(last_c, self.b + 1, self.b)
        nc = jnp.where(last_c, 0, self.c + 1)

        @pl.when(self.step + 1 < pl.num_programs(0) * self.n_chunks)
        def _():
            self._start(nb, nc, 1 - self.slot)

        for cp in self._copies(self.b, self.c, self.slot) + self.extra_copies(self.b, self.c, self.slot):
            cp.wait()
        return self.slot


class _CompressStream(_PageStream):
    def extra_copies(self, b, c, slot):
        n_pages = self.pt_ref.shape[1]
        nxt = jnp.minimum((c + 1) * self.P, n_pages - 1)
        return [pltpu.make_async_copy(self.pools[0].at[self.pt_ref[b, nxt]], self.bufs[0].at[slot, self.P],
                                      self.sem.at[1, slot])]


def _compress_paged_kernel(pt_ref, pool_ref, w1_ref, b1_ref, w2_ref, o_ref, buf_ref, sem_ref, rows_ref, *, pages):
    stride = NSA_CMP_STRIDE
    n_chunks = pages * PAGE_SIZE // stride
    slot = _CompressStream(pt_ref, [pool_ref], [buf_ref], sem_ref, pages).advance()
    for p in range(pages + 1):
        rows_ref[p * PAGE_SIZE:(p + 1) * PAGE_SIZE, :] = _page_rows(buf_ref, slot, p)
    acc = jnp.zeros((n_chunks, w1_ref.shape[-1]), f32)
    for p in range(0, NSA_CMP_LEN, 2):
        lhs = jnp.concatenate([rows_ref[pl.ds(p, n_chunks, stride=stride), :],
                               rows_ref[pl.ds(p + 1, n_chunks, stride=stride), :]], axis=1)
        acc = acc + _dot(lhs.astype(bf16), w1_ref[p // 2])
    hid = jax.nn.gelu(acc + b1_ref[...])
    o_ref[0] = _dot(hid.astype(bf16), w2_ref[...])


def _compress_weights(w1, b1, w2):
    G, dh = NSA_GROUPS, HEAD_DIM
    hidden = w1.shape[1]
    eye = jnp.eye(G, dtype=f32)
    w1p = w1.reshape(NSA_CMP_LEN, dh, hidden)
    w1bd = jnp.einsum("gh,pdn->pgdhn", eye, w1p).reshape(NSA_CMP_LEN // 2, 2 * G * dh, G * hidden).astype(bf16)
    w2bd = jnp.einsum("gh,nd->gnhd", eye, w2).reshape(G * hidden, G * dh).astype(bf16)
    return w1bd, jnp.tile(b1, G).reshape(1, G * hidden), w2bd


def _compress_paged(pool, page_table, w1, b1, w2, *, pages=32):
    B, n_pages = page_table.shape
    W = pool.shape[1] * pool.shape[2]
    assert pool.shape[3] == PAGE_SIZE and W == LANES
    assert n_pages % pages == 0 and PAGE_SIZE % NSA_CMP_STRIDE == 0 and NSA_CMP_LEN == 2 * NSA_CMP_STRIDE
    w1bd, b1bd, w2bd = _compress_weights(w1, b1, w2)
    C = pages * PAGE_SIZE // NSA_CMP_STRIDE
    n_total = n_pages * PAGE_SIZE // NSA_CMP_STRIDE
    const = lambda a: pl.BlockSpec(a.shape, lambda b, c, pt: (0,) * a.ndim)
    return pl.pallas_call(
        functools.partial(_compress_paged_kernel, pages=pages),
        grid_spec=pltpu.PrefetchScalarGridSpec(
            num_scalar_prefetch=1, grid=(B, n_pages // pages),
            in_specs=[pl.BlockSpec(memory_space=pl.ANY), const(w1bd), const(b1bd), const(w2bd)],
            out_specs=pl.BlockSpec((1, C, W), lambda b, c, pt: (b, c, 0)),
            scratch_shapes=[pltpu.VMEM((2, pages + 1) + pool.shape[1:], f32), pltpu.SemaphoreType.DMA((2, 2)),
                            pltpu.VMEM(((pages + 1) * PAGE_SIZE, W), f32)]),
        out_shape=jax.ShapeDtypeStruct((B, n_total, W), f32),
        compiler_params=_params("arbitrary", "arbitrary"), name="compress_paged")(page_table, pool, w1bd, b1bd, w2bd)


def _rank_rows(score, k, n_real):
    row = lax.broadcasted_iota(jnp.int32, (score.shape[0], 1), 0)
    rank = jnp.zeros(score.shape, jnp.int32)
    for i in range(n_real):
        si = score[i:i + 1, :]
        ahead = (si > score) | ((si == score) & (row > i))
        rank = rank + jnp.where(ahead, 1, 0)
    return rank < k


def _expand_lanes(a, e_ref):
    hi, lo = _split_hi_lo(a)
    return _dot(hi, e_ref[...]) + _dot(lo, e_ref[...])


def _merge_partials(sel, m, l, acc, e_ref):
    mx = jnp.max(jnp.where(sel, m, MASKED), axis=0, keepdims=True)
    w = jnp.where(sel, jnp.exp(m - mx), 0.0)
    den = jnp.sum(w * l, axis=0, keepdims=True)
    num = jnp.sum(_expand_lanes(w, e_ref) * acc, axis=0, keepdims=True)
    den_w = _expand_lanes(jnp.broadcast_to(den, (HEAD_ROWS, LANES)), e_ref)[0:1]
    return num / den_w


def _block_partial(s, v, e_ref):
    m = jnp.max(s, axis=0, keepdims=True)
    p = jnp.exp(s - m)
    l = jnp.sum(p, axis=0, keepdims=True)
    acc = jnp.sum(_dot(p.astype(bf16), e_ref[...]) * v, axis=0, keepdims=True)
    return m, l, acc


def _moba_decode_kernel(pt_ref, qs_ref, new_ref, kpool_ref, vpool_ref, e_ref, o_ref,
                        kbuf_ref, vbuf_ref, sem_ref, m_ref, l_ref, acc_ref, means_ref, *, pages, n_blk):
    blocks_per_chunk = pages * PAGE_SIZE // MOBA_BLOCK
    stream = _PageStream(pt_ref, [kpool_ref, vpool_ref], [kbuf_ref, vbuf_ref], sem_ref, pages)
    slot = stream.advance()
    c = stream.c
    qs = qs_ref[0]
    pages_per_block = MOBA_BLOCK // PAGE_SIZE
    pairs = kbuf_ref.shape[2] * kbuf_ref.shape[3] // LANES

    def block_rows(buf_ref, i):
        return jnp.concatenate(
            [jnp.concatenate([_page_rows(buf_ref, slot, i * pages_per_block + p, (2 * j, 2 * j + 2))
                              for j in range(pairs)], axis=1) for p in range(pages_per_block)], axis=0)

    for i in range(blocks_per_chunk):
        kb = block_rows(kbuf_ref, i)
        j = c * blocks_per_chunk + i
        means_ref[pl.ds(j, 1), :] = jnp.sum(kb, axis=0, keepdims=True) * (1.0 / MOBA_BLOCK)
        m, l, acc = _block_partial(_dot_nt(kb.astype(bf16), qs), block_rows(vbuf_ref, i), e_ref)
        m_ref[pl.ds(j, 1), :] = m
        l_ref[pl.ds(j, 1), :] = l
        acc_ref[pl.ds(j, 1), :] = acc

    @pl.when(c == stream.n_chunks - 1)
    def _():
        n_rows, W = acc_ref.shape
        tail = n_rows - n_blk
        k_new = jnp.broadcast_to(new_ref[0, 0:1, :], (HEAD_ROWS, W)).astype(bf16)
        s_new = _dot_nt(k_new, qs)[0:1]
        m_ref[pl.ds(n_blk, tail), :] = jnp.broadcast_to(s_new, (tail, LANES))
        l_ref[pl.ds(n_blk, tail), :] = jnp.ones((tail, LANES), f32)
        acc_ref[pl.ds(n_blk, tail), :] = jnp.broadcast_to(new_ref[0, 1:2, :], (tail, W))
        means_ref[pl.ds(n_blk, tail), :] = jnp.zeros((tail, W), f32)
        gate = _dot_nt(means_ref[...].astype(bf16), qs)
        row = lax.broadcasted_iota(jnp.int32, (n_rows, 1), 0)
        sel = (_rank_rows(gate, MOBA_TOPK, n_blk) & (row < n_blk)) | (row == n_blk)
        o_ref[0] = jnp.broadcast_to(_merge_partials(sel, m_ref[...], l_ref[...], acc_ref[...], e_ref), o_ref.shape[1:])


def _head_rows(q, n_heads, dtype):
    B = q.shape[0]
    dh = q.shape[1] // n_heads
    eye = jnp.eye(n_heads, dtype=q.dtype)
    rows = jnp.einsum("bhd,hg->bhgd", q.reshape(B, n_heads, dh), eye).reshape(B, n_heads, n_heads * dh)
    return jnp.pad(rows, ((0, 0), (0, LANES - n_heads), (0, 0))).astype(dtype)


def _head_expand(n_heads, dh):
    e = np.zeros((LANES, n_heads * dh), np.float32)
    for h in range(n_heads):
        e[h, h * dh:(h + 1) * dh] = 1.0
    return e


def _moba_decode(q, k_new, v_new, k_pool, v_pool, page_table, *, pages=8):
    B, W = q.shape
    n_pages = page_table.shape[1]
    past = n_pages * PAGE_SIZE
    n_blk = past // MOBA_BLOCK
    assert past % MOBA_BLOCK == 0 and n_pages % pages == 0 and (pages * PAGE_SIZE) % MOBA_BLOCK == 0
    assert MOBA_TOPK <= n_blk and k_pool.shape[1:] == (MOBA_HEADS, HEAD_DIM, PAGE_SIZE)
    qs = _head_rows(q * (HEAD_DIM ** -0.5), MOBA_HEADS, bf16)
    new = jnp.pad(jnp.stack([k_new, v_new], axis=1), ((0, 0), (0, 6), (0, 0)))
    e = jnp.asarray(_head_expand(MOBA_HEADS, HEAD_DIM), bf16)
    n_rows = _round_up(n_blk + 1, 8)
    per_b = lambda rows: pl.BlockSpec((1, rows, W), lambda b, c, pt: (b, 0, 0))
    out = pl.pallas_call(
        functools.partial(_moba_decode_kernel, pages=pages, n_blk=n_blk),
        grid_spec=pltpu.PrefetchScalarGridSpec(
            num_scalar_prefetch=1, grid=(B, n_pages // pages),
            in_specs=[per_b(LANES), per_b(8), pl.BlockSpec(memory_space=pl.ANY),
                      pl.BlockSpec(memory_space=pl.ANY), pl.BlockSpec(e.shape, lambda b, c, pt: (0, 0))],
            out_specs=per_b(8),
            scratch_shapes=[pltpu.VMEM((2, pages) + k_pool.shape[1:], f32), pltpu.VMEM((2, pages) + v_pool.shape[1:], f32),
                            pltpu.SemaphoreType.DMA((2, 2)),
                            pltpu.VMEM((n_rows, LANES), f32), pltpu.VMEM((n_rows, LANES), f32),
                            pltpu.VMEM((n_rows, W), f32), pltpu.VMEM((n_rows, W), f32)]),
        out_shape=jax.ShapeDtypeStruct((B, 8, W), f32),
        compiler_params=_params("arbitrary", "arbitrary"), name="moba_decode")(
            page_table, qs, new, k_pool, v_pool, e)
    return out[:, 0]


def _nsa_decode_kernel(pt_ref, qn_ref, g_ref, new_ref, ck_ref, cv_ref, wk_ref, wv_ref, kpool_ref, vpool_ref,
                       imp_ref, e_ref, eg_ref, eye_ref, o_ref,
                       kbuf_ref, vbuf_ref, sem_ref, m_ref, l_ref, acc_ref, selT_ref, oc_ref, ow_ref,
                       *, pages, past, n_sel):
    rep = NSA_REP
    blocks_per_page = PAGE_SIZE // NSA_SEL_BLOCK
    stream = _PageStream(pt_ref, [kpool_ref, vpool_ref], [kbuf_ref, vbuf_ref], sem_ref, pages)
    slot = stream.advance()
    c = stream.c
    qn = qn_ref[0].astype(bf16)
    q8 = qn_ref[0, 0:HEAD_ROWS, :].astype(bf16)
    qpos = past
    n_rows = acc_ref.shape[0]
    lane = lax.broadcasted_iota(jnp.int32, (1, LANES), 1)
    tile4 = lambda a: jnp.concatenate([a] * rep, axis=1)

    @pl.when(c == 0)
    def _():
        n_cmp = ck_ref.shape[1]
        cmp_end = lax.broadcasted_iota(jnp.int32, (1, n_cmp), 1) * NSA_CMP_STRIDE + (NSA_CMP_LEN - 1)
        s = jnp.where(cmp_end <= qpos, _dot_nt(q8, ck_ref[0].astype(bf16)), MASKED)
        m = jnp.maximum(jnp.max(s, axis=-1, keepdims=True), ROW_MAX_INIT)
        e = jnp.exp(s - m)
        den = jnp.sum(e, axis=-1, keepdims=True)
        p_c = e / jnp.where(den > 0, den, 1.0)
        oc_ref[...] = _dot(p_c.astype(bf16), cv_ref[0].astype(bf16))
        row8 = lax.broadcasted_iota(jnp.int32, (HEAD_ROWS, 1), 0)
        p_grp = jnp.where(row8 < rep, jnp.sum(p_c[0:rep], axis=0, keepdims=True),
                          jnp.sum(p_c[rep:2 * rep], axis=0, keepdims=True))
        imp = _dot(p_grp.astype(bf16), imp_ref[...])
        n_sel_pad = imp.shape[1]
        jblk = lax.broadcasted_iota(jnp.int32, (1, n_sel_pad), 1)
        cur = qpos // NSA_SEL_BLOCK
        visible = jblk <= cur
        forced = (jblk == 0) | (jblk == cur) | (jblk == cur - 1)
        score = jnp.where(visible, jnp.where(forced, FORCED, imp), -FORCED)
        sel = _topk_mask(score, NSA_TOPN, jblk, n_sel) & visible
        sel_rows = jnp.concatenate([jnp.where(sel, 1.0, 0.0), jnp.zeros((LANES - HEAD_ROWS, n_sel_pad), f32)], axis=0)
        selT_ref[...] = _dot_nt(eye_ref[...], sel_rows.astype(bf16))
        n_w = wk_ref.shape[-1]
        wk = wk_ref[0].reshape(LANES, n_w).astype(bf16)
        wv = wv_ref[0].reshape(LANES, n_w).astype(bf16)
        kp = (past - n_w) + lax.broadcasted_iota(jnp.int32, (1, n_w), 1)
        wmask = (kp <= qpos) & (kp > qpos - NSA_WINDOW)
        kw_new = new_ref[0, 2:3, :].astype(bf16).astype(f32)
        vw_new = new_ref[0, 3:4, :].astype(bf16).astype(f32)
        s_n = jnp.sum(q8.astype(f32) * kw_new, axis=-1, keepdims=True)
        s_w = jnp.where(wmask, _dot(q8, wk), MASKED)
        m = jnp.maximum(jnp.max(s_w, axis=-1, keepdims=True), s_n)
        e = jnp.exp(s_w - m)
        e_n = jnp.exp(s_n - m)
        den = jnp.sum(e, axis=-1, keepdims=True) + e_n
        ow_ref[...] = (_dot_nt(e.astype(bf16), wv) + e_n.astype(bf16).astype(f32) * vw_new) / den
        tail = n_rows - (n_sel - 1)
        k_new = jnp.broadcast_to(new_ref[0, 0:1, :], (HEAD_ROWS, LANES)).astype(bf16)
        m_ref[pl.ds(n_sel - 1, tail), :] = jnp.broadcast_to(_dot_nt(k_new, qn)[0:1], (tail, LANES))
        l_ref[pl.ds(n_sel - 1, tail), :] = jnp.ones((tail, LANES), f32)
        acc_ref[pl.ds(n_sel - 1, tail), :] = jnp.broadcast_to(tile4(new_ref[0, 1:2, :]), (tail, acc_ref.shape[1]))

    def page_body(pg, carry):
        s = _dot_nt(_page_rows(kbuf_ref, slot, pg).astype(bf16), qn)
        vp = _page_rows(vbuf_ref, slot, pg)
        for i in range(blocks_per_page):
            rows = slice(i * NSA_SEL_BLOCK, (i + 1) * NSA_SEL_BLOCK)
            m, l, acc = _block_partial(s[rows], tile4(vp[rows]), e_ref)
            j = (c * pages + pg) * blocks_per_page + i
            m_ref[pl.ds(j, 1), :] = m
            l_ref[pl.ds(j, 1), :] = l
            acc_ref[pl.ds(j, 1), :] = acc
        return carry

    lax.fori_loop(0, pages, page_body, 0, unroll=4)

    @pl.when(c == stream.n_chunks - 1)
    def _():
        o_s = _merge_partials(selT_ref[0:n_rows, :] > 0.5, m_ref[...], l_ref[...], acc_ref[...], e_ref)
        wide = lambda o8: jnp.concatenate(
            [jnp.where(lane < HEAD_DIM, o8[r:r + 1], o8[rep + r:rep + r + 1]) for r in range(rep)], axis=1)
        gate = [_expand_lanes(g_ref[0], eg_ref.at[i])[0:1] for i in range(3)]
        o = gate[0] * wide(oc_ref[...]) + gate[1] * o_s + gate[2] * wide(ow_ref[...])
        o_ref[0] = jnp.broadcast_to(o, o_ref.shape[1:])


def _nsa_decode(q, gates, new_rows, c_k, c_v, win_k, win_v, k_pool, v_pool, page_table, *, pages=32):
    B = q.shape[0]
    G, R, d = NSA_GROUPS, NSA_REP, HEAD_DIM
    n_pages = page_table.shape[1]
    past = n_pages * PAGE_SIZE
    n_cmp = c_k.shape[1]
    n_sel = max(-(-(past + 1) // NSA_SEL_BLOCK), NSA_TOPN)
    n_sel_pad = _round_up(n_sel, LANES)
    n_rows = _round_up(n_sel, 8)
    assert past % NSA_SEL_BLOCK == 0 and n_pages % pages == 0 and past >= win_k.shape[-1] and G == 2
    assert k_pool.shape[1:] == (G, d, PAGE_SIZE) and win_k.shape[1:3] == (G, d)
    assert n_cmp == past // NSA_CMP_STRIDE and n_sel - 1 == past // NSA_SEL_BLOCK
    qg = jnp.einsum("bgrd,gk->bgrkd", q.reshape(B, G, R, d) * (d ** -0.5), jnp.eye(G, dtype=f32))
    qn = jnp.pad(qg.reshape(B, NSA_HEADS, LANES), ((0, 0), (0, LANES - NSA_HEADS), (0, 0)))
    g8 = jnp.broadcast_to(gates[:, None, :], (B, HEAD_ROWS, LANES))
    new8 = jnp.pad(new_rows, ((0, 0), (0, 8 - new_rows.shape[1]), (0, 0)))
    imp = np.zeros((n_cmp, n_sel_pad), np.float32)
    imp[:, :n_sel] = _nsa_static_maps(n_cmp, n_sel)
    e_np = np.zeros((LANES, NSA_Q_W), np.float32)
    eg_np = np.zeros((3, LANES, NSA_Q_W), np.float32)
    for g in range(G):
        for r in range(R):
            h = g * R + r
            cols = slice(r * LANES + g * d, r * LANES + (g + 1) * d)
            e_np[h, cols] = 1.0
            for i in range(3):
                eg_np[i, 3 * h + i, cols] = 1.0
    consts = [jnp.asarray(imp, bf16), jnp.asarray(e_np, bf16), jnp.asarray(eg_np, bf16),
              jnp.asarray(np.eye(n_sel_pad, dtype=np.float32), bf16)]
    per_b = lambda a: pl.BlockSpec((1,) + a.shape[1:], lambda b, c, pt: (b,) + (0,) * (a.ndim - 1))
    const = lambda a: pl.BlockSpec(a.shape, lambda b, c, pt: (0,) * a.ndim)
    anyspec = pl.BlockSpec(memory_space=pl.ANY)
    per_b_in = [qn, g8, new8, c_k, c_v, win_k, win_v]
    out = pl.pallas_call(
        functools.partial(_nsa_decode_kernel, pages=pages, past=past, n_sel=n_sel),
        grid_spec=pltpu.PrefetchScalarGridSpec(
            num_scalar_prefetch=1, grid=(B, n_pages // pages),
            in_specs=[per_b(a) for a in per_b_in] + [anyspec, anyspec] + [const(a) for a in consts],
            out_specs=pl.BlockSpec((1, 8, NSA_Q_W), lambda b, c, pt: (b, 0, 0)),
            scratch_shapes=[pltpu.VMEM((2, pages) + k_pool.shape[1:], f32), pltpu.VMEM((2, pages) + v_pool.shape[1:], f32),
                            pltpu.SemaphoreType.DMA((2, 2)),
                            pltpu.VMEM((n_rows, LANES), f32), pltpu.VMEM((n_rows, LANES), f32),
                            pltpu.VMEM((n_rows, NSA_Q_W), f32), pltpu.VMEM((n_sel_pad, LANES), f32),
                            pltpu.VMEM((HEAD_ROWS, LANES), f32), pltpu.VMEM((HEAD_ROWS, LANES), f32)]),
        out_shape=jax.ShapeDtypeStruct((B, 8, NSA_Q_W), f32),
        compiler_params=_params("arbitrary", "arbitrary"), name="nsa_decode")(
            page_table, *per_b_in, k_pool, v_pool, *consts)
    return out[:, 0]


def _mem_decode_kernel(q_ref, k_ref, v_ref, o_ref):
    d = MEM_HEAD_DIM
    s = _dot_nt(q_ref[0].astype(bf16), k_ref[0].astype(bf16)) * (d ** -0.5)
    e = jnp.exp(s - jnp.max(s, axis=-1, keepdims=True))
    p = e / jnp.sum(e, axis=-1, keepdims=True)
    o8 = _dot(p.astype(bf16), v_ref[0].astype(bf16))
    own = jnp.right_shift(lax.broadcasted_iota(jnp.int32, (1, o8.shape[1]), 1), d.bit_length() - 1) == \
        lax.broadcasted_iota(jnp.int32, (HEAD_ROWS, 1), 0)
    o_ref[0] = jnp.broadcast_to(jnp.sum(jnp.where(own, o8, 0.0), axis=0, keepdims=True), o_ref.shape[1:])


def _mem_decode(q, mk, mv):
    B, W = q.shape
    M = mk.shape[1]
    q8 = _head_rows(q.astype(f32), MEM_HEADS, f32)[:, :HEAD_ROWS]
    out = pl.pallas_call(
        _mem_decode_kernel, grid=(B,),
        in_specs=[pl.BlockSpec((1, HEAD_ROWS, W), lambda b: (b, 0, 0)), pl.BlockSpec((1, M, W), lambda b: (b, 0, 0)),
                  pl.BlockSpec((1, M, W), lambda b: (b, 0, 0))],
        out_specs=pl.BlockSpec((1, 8, W), lambda b: (b, 0, 0)),
        out_shape=jax.ShapeDtypeStruct((B, 8, W), f32),
        compiler_params=_params("parallel"), name="mem_decode")(q8, mk, mv)
    return out[:, 0]


def _prompt_layer(x, mem, lw):
    B, S, D = x.shape
    pos = jnp.arange(S, dtype=jnp.int32)
    pr = _project(x.reshape(B * S, D), pos, lw, head_order=NSA_HEAD_ORDER, q_dtype=bf16, q_scale=HEAD_DIM ** -0.5,
                  dim_major=("k_cmp", "k_cmp+", "v_cmp", "v_cmp+", "k_slc", "v_slc", "k_win", "v_win", "moba_k", "moba_v"))
    seq = lambda name: pr[name].reshape(B, S, -1)
    Bm, M, _ = mem.shape
    kv = _matmul(mem.reshape(Bm * M, D), lw["w_mem_kv"].astype(bf16), gain=lw["norm_mem"], name="mem_kv")
    kv = kv.reshape(Bm, M, 2 * MEM_W)
    c_k = _compress(seq("k_cmp"), lw["cmp_w1_k"], lw["cmp_b1_k"], lw["cmp_w2_k"], chunks_per_step=128)
    c_v = _compress(seq("v_cmp"), lw["cmp_w1_v"], lw["cmp_b1_v"], lw["cmp_w2_v"], chunks_per_step=128)
    o_nsa = _nsa_prompt(seq("nsa_q"), seq("nsa_g"), c_k, c_v, pr["k_slc_t"], pr["v_slc_t"], pr["k_win_t"], pr["v_win_t"])
    o_moba = _moba_prompt(seq("moba_q"), pr["moba_k_t"], pr["moba_v_t"])
    o_mem = _mem_attn(seq("mem_q"), kv)
    merged = _merge(x.reshape(B * S, D), o_nsa.reshape(B * S, -1), o_moba.reshape(B * S, -1),
                    o_mem.reshape(B * S, -1), lw, nsa_head_order=NSA_HEAD_ORDER)
    wb = min(NSA_WINDOW, S)
    heads = lambda name, n, d: jnp.transpose(pr[name + "_t"].reshape(B, n, d, S), (0, 3, 1, 2))
    state = (heads("k_cmp", NSA_GROUPS, HEAD_DIM), heads("v_cmp", NSA_GROUPS, HEAD_DIM),
             heads("k_slc", NSA_GROUPS, HEAD_DIM), heads("v_slc", NSA_GROUPS, HEAD_DIM),
             heads("k_win", NSA_GROUPS, HEAD_DIM)[:, S - wb:], heads("v_win", NSA_GROUPS, HEAD_DIM)[:, S - wb:],
             heads("moba_k", MOBA_HEADS, HEAD_DIM), heads("moba_v", MOBA_HEADS, HEAD_DIM),
             kv[:, :, :MEM_W].reshape(Bm, M, MEM_HEADS, MEM_HEAD_DIM),
             kv[:, :, MEM_W:].reshape(Bm, M, MEM_HEADS, MEM_HEAD_DIM))
    return merged, state


def _sample_layer(x, c_cmp_k, c_cmp_v, c_slc_k, c_slc_v, c_win_k, c_win_v,
                  c_moba_k, c_moba_v, c_mem_k, c_mem_v, page_table, lw):
    B, T, D = x.shape
    past_len = page_table.shape[1] * PAGE_SIZE
    assert T == 1 and past_len % NSA_CMP_STRIDE == 0
    pos = jnp.full((B,), past_len, jnp.int32)
    pr = _project(x.reshape(B, D), pos, lw, head_order=STD_HEAD_ORDER, q_dtype=f32, q_scale=1.0)
    flat = lambda a: a.reshape(a.shape[0], a.shape[1], -1)
    pages = _cache_pages
    c_k = _compress_paged(pages(c_cmp_k), page_table, lw["cmp_w1_k"], lw["cmp_b1_k"], lw["cmp_w2_k"])
    c_v = _compress_paged(pages(c_cmp_v), page_table, lw["cmp_w1_v"], lw["cmp_b1_v"], lw["cmp_w2_v"])
    new_rows = jnp.stack([pr["k_slc"], pr["v_slc"], pr["k_win"], pr["v_win"]], axis=1)
    o_nsa = _nsa_decode(pr["nsa_q"], pr["nsa_g"], new_rows, c_k, c_v, pages(c_win_k), pages(c_win_v),
                        pages(c_slc_k), pages(c_slc_v), page_table)
    o_moba = _moba_decode(pr["moba_q"], pr["moba_k"], pr["moba_v"], pages(c_moba_k), pages(c_moba_v), page_table)
    o_mem = _mem_decode(pr["mem_q"], flat(c_mem_k), flat(c_mem_v))
    merged = _merge(x.reshape(B, D), o_nsa.astype(bf16), o_moba.astype(bf16), o_mem.astype(bf16), lw,
                    nsa_head_order=NSA_HEAD_ORDER)
    heads = lambda name, n, d: pr[name].reshape(B, T, n, d)
    wb = c_win_k.shape[1]
    roll_in = lambda cache, name: jnp.concatenate([cache, heads(name, NSA_GROUPS, HEAD_DIM)], axis=1)[:, -wb:]
    state = (heads("k_cmp", NSA_GROUPS, HEAD_DIM), heads("v_cmp", NSA_GROUPS, HEAD_DIM),
             heads("k_slc", NSA_GROUPS, HEAD_DIM), heads("v_slc", NSA_GROUPS, HEAD_DIM),
             roll_in(c_win_k, "k_win"), roll_in(c_win_v, "v_win"),
             heads("moba_k", MOBA_HEADS, HEAD_DIM), heads("moba_v", MOBA_HEADS, HEAD_DIM))
    return merged, state


def kernel(x_prompt, x_sample, mem_prompt, cache_nsa_cmp_k, cache_nsa_cmp_v, cache_nsa_slc_k, cache_nsa_slc_v, cache_nsa_win_k, cache_nsa_win_v, cache_moba_k, cache_moba_v, cache_mem_k, cache_mem_v, page_table, norm_mix, norm_mem, w_in, w_mem_kv, cmp_w1_k, cmp_b1_k, cmp_w2_k, cmp_w1_v, cmp_b1_v, cmp_w2_v, w_nsa_out, w_moba_out, w_mem_out, w_merge_gate, b_merge_gate, w_out, norm_ffn, w_group, b_group, w_router, b_router, w_expert_gate, w_expert_up, w_expert_down, norm_final):
    depth = norm_mix.shape[0]
    xp, xs = x_prompt, x_sample
    Bp, S, D = xp.shape
    Bs, T, _ = xs.shape
    n_p = Bp * S
    p_layers, s_layers = [], []
    for l in range(depth):
        lw = dict(norm_mix=norm_mix[l], norm_mem=norm_mem[l], w_in=w_in[l], w_mem_kv=w_mem_kv[l],
                  cmp_w1_k=cmp_w1_k[l], cmp_b1_k=cmp_b1_k[l], cmp_w2_k=cmp_w2_k[l],
                  cmp_w1_v=cmp_w1_v[l], cmp_b1_v=cmp_b1_v[l], cmp_w2_v=cmp_w2_v[l],
                  w_nsa_out=w_nsa_out[l], w_moba_out=w_moba_out[l], w_mem_out=w_mem_out[l],
                  w_merge_gate=w_merge_gate[l], b_merge_gate=b_merge_gate[l], w_out=w_out[l],
                  norm_ffn=norm_ffn[l], w_group=w_group[l], b_group=b_group[l],
                  w_router=w_router[l], b_router=b_router[l], w_expert_gate=w_expert_gate[l],
                  w_expert_up=w_expert_up[l], w_expert_down=w_expert_down[l])
        (xp_mid, hp, lp), p_new = _prompt_layer(xp, mem_prompt, lw)
        (xs_mid, hs, ls), s_new = _sample_layer(xs, cache_nsa_cmp_k[l], cache_nsa_cmp_v[l], cache_nsa_slc_k[l],
                                                cache_nsa_slc_v[l], cache_nsa_win_k[l], cache_nsa_win_v[l],
                                                cache_moba_k[l], cache_moba_v[l], cache_mem_k[l], cache_mem_v[l],
                                                page_table, lw)
        ys, dest, weight = _moe_rows([hp, hs], jnp.concatenate([lp, ls], axis=0), lw)
        last = l == depth - 1
        xp = _combine(xp_mid, ys, dest[:n_p], weight[:n_p], norm_final, normalise=last).reshape(Bp, S, D)
        xs = _combine(xs_mid, ys, dest[n_p:], weight[n_p:], norm_final, normalise=last).reshape(Bs, T, D)
        p_layers.append(p_new)
        s_layers.append(s_new)
    p_out = [jnp.stack(z) for z in zip(*p_layers)]
    s_out = [jnp.stack(z) for z in zip(*s_layers)]
    return (xp, xs, *p_out, *s_out)
```
